```python
import functools
import jax, jax.numpy as jnp
from jax import lax
import numpy as np

D_MODEL = 1024
BATCH = 4
SEQ = 4096
DEPTH = 1
DEC_BATCH = 128
DEC_SEQ = 4
PAST_LEN = 16384
PAGE_SIZE = 128

N_HEADS = 16
HEAD_DIM = 64
N_KV_HEADS = 4
GROUP = N_HEADS // N_KV_HEADS
WINDOW = 128
ATTN_BLOCK = 128
ROPE_THETA = 10000.0
Q_W = N_HEADS * HEAD_DIM
KV_W = N_KV_HEADS * HEAD_DIM
D_CONV = 1024
CONV_WIDTH = 3
PEER_HEADS = 8
N_KEYS = 128
N_EXPERTS = N_KEYS * N_KEYS
PEER_TOPK = 16
PEER_QDIM = 256
PEER_HALF = PEER_QDIM // 2
PEER_BLOCK = 128
IN_SIZES = (Q_W, KV_W, KV_W, D_CONV, D_CONV, D_CONV, D_MODEL, D_MODEL)
D_IN = Q_W + 2 * KV_W + 3 * D_CONV + 2 * D_MODEL
EPS = 1e-6
NEG = -1e30

kernel_name = 'hybrid_swa_shortconv_peer_adaln_step'


def rms_norm(x, g):
    xf = x.astype(jnp.float32)
    y = xf * lax.rsqrt(jnp.mean(xf * xf, axis=-1, keepdims=True) + EPS)
    return (y * g.astype(jnp.float32)).astype(x.dtype)


def rope(x, pos):
    half = HEAD_DIM // 2
    inv = jnp.power(jnp.float32(ROPE_THETA), -jnp.arange(half, dtype=jnp.float32) / half)
    ang = pos.astype(jnp.float32)[:, None] * inv[None, :]
    cos = jnp.cos(ang)[:, None, :]
    sin = jnp.sin(ang)[:, None, :]
    x1 = x[..., :half].astype(jnp.float32)
    x2 = x[..., half:].astype(jnp.float32)
    return jnp.concatenate([x1 * cos - x2 * sin, x2 * cos + x1 * sin], axis=-1).astype(x.dtype)


def in_projection(h, w_in, q_norm, k_norm, pos):
    z = h @ w_in
    parts = []
    start = 0
    for n in IN_SIZES:
        parts.append(z[..., start:start + n])
        start += n
    q, k, v, b_gate, c_gate, x_conv, g_attn, g_conv = parts
    bsz, s = h.shape[0], h.shape[1]
    q = rope(rms_norm(q.reshape(bsz, s, N_HEADS, HEAD_DIM), q_norm), pos)
    k = rope(rms_norm(k.reshape(bsz, s, N_KV_HEADS, HEAD_DIM), k_norm), pos)
    v = v.reshape(bsz, s, N_KV_HEADS, HEAD_DIM)
    u = c_gate * x_conv
    return q, k, v, u, b_gate, g_attn, g_conv


def sink_attend(q, k, v, valid, sinks):
    s = jnp.einsum('...qkgd,...skd->...kgqs', q, k).astype(jnp.float32) * (HEAD_DIM ** -0.5)
    s = jnp.where(valid, s, jnp.float32(NEG))
    sink = sinks.astype(jnp.float32).reshape(N_KV_HEADS, GROUP)[:, :, None]
    m = jnp.maximum(jnp.max(s, axis=-1), sink)
    p = jnp.exp(s - m[..., None])
    denom = jnp.sum(p, axis=-1) + jnp.exp(sink - m)
    probs = (p / denom[..., None]).astype(v.dtype)
    return jnp.einsum('...kgqs,...skd->...qkgd', probs, v)


def swa_prompt(q, k, v, sinks):
    bsz, s = q.shape[0], q.shape[1]
    nb = s // ATTN_BLOCK
    qb = q.reshape(bsz, nb, ATTN_BLOCK, N_KV_HEADS, GROUP, HEAD_DIM)
    kb = k.reshape(bsz, nb, ATTN_BLOCK, N_KV_HEADS, HEAD_DIM)
    vb = v.reshape(bsz, nb, ATTN_BLOCK, N_KV_HEADS, HEAD_DIM)

    def band(t):
        prev = jnp.concatenate([jnp.zeros_like(t[:, :1]), t[:, :-1]], axis=1)
        return jnp.concatenate([prev, t], axis=2)

    blk = jnp.arange(nb)[:, None]
    qpos = blk * ATTN_BLOCK + jnp.arange(ATTN_BLOCK)[None, :]
    kpos = (blk - 1) * ATTN_BLOCK + jnp.arange(2 * ATTN_BLOCK)[None, :]
    diff = qpos[:, :, None] - kpos[:, None, :]
    valid = (diff >= 0) & (diff <= WINDOW) & (kpos[:, None, :] >= 0)
    o = sink_attend(qb, band(kb), band(vb), valid[:, None, None], sinks)
    return o.reshape(bsz, s, Q_W)


def swa_sample(q, k, v, win_k, win_v, sinks):
    bsz, t = q.shape[0], q.shape[1]
    L = win_k.shape[1]
    ka = jnp.concatenate([win_k, k], axis=1)
    va = jnp.concatenate([win_v, v], axis=1)
    qpos = PAST_LEN + jnp.arange(t)
    kpos = PAST_LEN - L + jnp.arange(L + t)
    diff = qpos[:, None] - kpos[None, :]
    valid = (diff >= 0) & (diff <= WINDOW)
    o = sink_attend(q.reshape(bsz, t, N_KV_HEADS, GROUP, HEAD_DIM), ka, va, valid, sinks)
    return o.reshape(bsz, t, Q_W), ka[:, -L:], va[:, -L:]


def causal_conv(ext, w, t):
    y = ext[:, 0:t] * w[0]
    for j in range(1, CONV_WIDTH):
        y = y + ext[:, j:j + t] * w[j]
    return y


def peer(h, w_pq, sub_keys, u, v):
    lead = h.shape[:-1]
    tok = h.reshape(-1, D_MODEL)
    n = tok.shape[0]
    nb = -(-n // PEER_BLOCK)
    tok = jnp.pad(tok, ((0, nb * PEER_BLOCK - n), (0, 0))).reshape(nb, PEER_BLOCK, D_MODEL)

    def block(tb):
        q = (tb @ w_pq).reshape(PEER_BLOCK, PEER_HEADS, 2, PEER_HALF)
        s = jnp.einsum('thpd,hpkd->thpk', q, sub_keys)
        sv, si = lax.top_k(s, PEER_TOPK)
        cand = (sv[:, :, 0, :, None] + sv[:, :, 1, None, :]).reshape(PEER_BLOCK, PEER_HEADS, PEER_TOPK * PEER_TOPK)
        cs, ci = lax.top_k(cand, PEER_TOPK)
        i1 = jnp.take_along_axis(si[:, :, 0], ci // PEER_TOPK, axis=-1)
        i2 = jnp.take_along_axis(si[:, :, 1], ci % PEER_TOPK, axis=-1)
        e = i1 * N_KEYS + i2
        g = jax.nn.softmax(cs.astype(jnp.float32), axis=-1).astype(tb.dtype)
        ue = jnp.take(u, e, axis=0)
        a = jax.nn.gelu(jnp.einsum('td,thkd->thk', tb, ue))
        ve = jnp.take(v, e, axis=0)
        return jnp.einsum('thk,thkd->td', g * a, ve)

    out = lax.map(block, tok)
    return out.reshape(-1, D_MODEL)[:n].reshape(lead + (D_MODEL,))


def trunk_layer(x, c, pos, win_k, win_v, conv_buf, w_ada, b_ada, norm1, w_in, q_norm, k_norm,
                sinks, conv_w, w_attn_o, w_conv_o, w_out, norm2, w_peer_q, sub_keys, peer_u, peer_v):
    mod = (jax.nn.silu(c) @ w_ada + b_ada)[:, None, :]
    sh1, sc1, gt1, sh2, sc2, gt2 = jnp.split(mod, 6, axis=-1)
    t = x.shape[1]
    h = rms_norm(x, norm1) * (1 + sc1) + sh1
    q, k, v, u, b_gate, g_attn, g_conv = in_projection(h, w_in, q_norm, k_norm, pos)
    if win_k is None:
        attn = swa_prompt(q, k, v, sinks)
        new_k, new_v = k[:, -WINDOW:], v[:, -WINDOW:]
        ext = jnp.pad(u, ((0, 0), (CONV_WIDTH - 1, 0), (0, 0)))
    else:
        attn, new_k, new_v = swa_sample(q, k, v, win_k, win_v, sinks)
        ext = jnp.concatenate([conv_buf, u], axis=1)
    conv_y = causal_conv(ext, conv_w, t)
    new_conv = ext[:, -(CONV_WIDTH - 1):]
    merged = jax.nn.sigmoid(g_attn) * (attn @ w_attn_o) + jax.nn.sigmoid(g_conv) * ((b_gate * conv_y) @ w_conv_o)
    x = x + gt1 * (merged @ w_out)
    h2 = rms_norm(x, norm2) * (1 + sc2) + sh2
    x = x + gt2 * peer(h2, w_peer_q, sub_keys, peer_u, peer_v)
    return x, new_k, new_v, new_conv


def setup_inputs(seed: int = 0) -> dict:
    key = jax.random.key(seed)
    ks = jax.random.split(key, 24)
    f = jnp.float32
    nrm = lambda k, shp, s: jax.random.normal(k, shp, f) * s
    return {
        'x_prompt': nrm(ks[0], (BATCH, SEQ, D_MODEL), 1.0),
        'x_sample': nrm(ks[1], (DEC_BATCH, DEC_SEQ, D_MODEL), 1.0),
        'c_prompt': nrm(ks[2], (BATCH, D_MODEL), 1.0),
        'c_sample': nrm(ks[3], (DEC_BATCH, D_MODEL), 1.0),
        'cache_win_k': nrm(ks[4], (DEPTH, DEC_BATCH, WINDOW, N_KV_HEADS, HEAD_DIM), 1.0),
        'cache_win_v': nrm(ks[5], (DEPTH, DEC_BATCH, WINDOW, N_KV_HEADS, HEAD_DIM), 1.0),
        'state_conv': nrm(ks[6], (DEPTH, DEC_BATCH, CONV_WIDTH - 1, D_CONV), 0.5),
        'w_ada': nrm(ks[7], (DEPTH, D_MODEL, 6 * D_MODEL), 0.5 * D_MODEL ** -0.5),
        'b_ada': nrm(ks[8], (DEPTH, 6 * D_MODEL), 0.01),
        'norm1': 1.0 + nrm(ks[9], (DEPTH, D_MODEL), 0.02),
        'w_in': nrm(ks[10], (DEPTH, D_MODEL, D_IN), D_MODEL ** -0.5),
        'q_norm': 1.0 + nrm(ks[11], (DEPTH, HEAD_DIM), 0.02),
        'k_norm': 1.0 + nrm(ks[12], (DEPTH, HEAD_DIM), 0.02),
        'attn_sinks': nrm(ks[13], (DEPTH, N_HEADS), 0.5),
        'conv_w': nrm(ks[14], (DEPTH, CONV_WIDTH, D_CONV), CONV_WIDTH ** -0.5),
        'w_attn_o': nrm(ks[15], (DEPTH, Q_W, D_MODEL), Q_W ** -0.5),
        'w_conv_o': nrm(ks[16], (DEPTH, D_CONV, D_MODEL), D_CONV ** -0.5),
        'w_out': nrm(ks[17], (DEPTH, D_MODEL, D_MODEL), D_MODEL ** -0.5),
        'norm2': 1.0 + nrm(ks[18], (DEPTH, D_MODEL), 0.02),
        'w_peer_q': nrm(ks[19], (DEPTH, D_MODEL, PEER_HEADS * PEER_QDIM), D_MODEL ** -0.5),
        'peer_sub_keys': nrm(ks[20], (DEPTH, PEER_HEADS, 2, N_KEYS, PEER_HALF), PEER_HALF ** -0.5),
        'peer_u': nrm(ks[21], (DEPTH, N_EXPERTS, D_MODEL), D_MODEL ** -0.5),
        'peer_v': nrm(ks[22], (DEPTH, N_EXPERTS, D_MODEL), PEER_HEADS ** -0.5),
    }


def reference(x_prompt, x_sample, c_prompt, c_sample, cache_win_k, cache_win_v, state_conv,
              w_ada, b_ada, norm1, w_in, q_norm, k_norm, attn_sinks, conv_w, w_attn_o, w_conv_o,
              w_out, norm2, w_peer_q, peer_sub_keys, peer_u, peer_v):
    pos_p = jnp.arange(x_prompt.shape[1])
    pos_s = PAST_LEN + jnp.arange(x_sample.shape[1])
    yp, ys = x_prompt, x_sample
    kp_l, vp_l, cp_l, ks_l, vs_l, cs_l = [], [], [], [], [], []
    for l in range(DEPTH):
        lw = (w_ada[l], b_ada[l], norm1[l], w_in[l], q_norm[l], k_norm[l], attn_sinks[l], conv_w[l],
              w_attn_o[l], w_conv_o[l], w_out[l], norm2[l], w_peer_q[l], peer_sub_keys[l], peer_u[l], peer_v[l])
        yp, kp, vp, cp = trunk_layer(yp, c_prompt, pos_p, None, None, None, *lw)
        ys, kss, vss, css = trunk_layer(ys, c_sample, pos_s, cache_win_k[l], cache_win_v[l], state_conv[l], *lw)
        kp_l.append(kp); vp_l.append(vp); cp_l.append(cp)
        ks_l.append(kss); vs_l.append(vss); cs_l.append(css)
    win_k_prompt = jnp.stack(kp_l)
    win_v_prompt = jnp.stack(vp_l)
    conv_prompt = jnp.stack(cp_l)
    win_k_sample = jnp.stack(ks_l)
    win_v_sample = jnp.stack(vs_l)
    conv_sample = jnp.stack(cs_l)
    return (yp, ys, win_k_prompt, win_v_prompt, conv_prompt, win_k_sample, win_v_sample, conv_sample)
```

```python
import functools

import numpy as np
import jax
import jax.numpy as jnp
from jax import lax
from jax.experimental import pallas as pl
from jax.experimental.pallas import tpu as pltpu

F32 = jnp.float32
BF16 = jnp.bfloat16
HIGHEST = lax.Precision.HIGHEST

D_MODEL = 1024
N_HEADS = 16
HEAD_DIM = 64
N_KV_HEADS = 4
GROUP = N_HEADS // N_KV_HEADS
WINDOW = 128
ATTN_BLOCK = 128
ROPE_THETA = 10000.0
Q_W = N_HEADS * HEAD_DIM
KV_W = N_KV_HEADS * HEAD_DIM
D_CONV = 1024
CONV_WIDTH = 3
PEER_HEADS = 8
N_KEYS = 128
PEER_TOPK = 16
PEER_HALF = 128
PEER_SEL = PEER_HEADS * PEER_TOPK
PAST_LEN = 16384
EPS = 1e-6
NEG = -1e30
IN_SIZES = (Q_W, KV_W, KV_W, D_CONV, D_CONV, D_CONV, D_MODEL, D_MODEL)
IN_OFFS = tuple(int(v) for v in np.cumsum((0,) + IN_SIZES))
D_IN = IN_OFFS[-1]

LANES = 128
SUBLANES = 8
VMEM_LIMIT = 56 * 1024 * 1024
PEER_TOK = 8
PEER_PAIRS = tuple((i, j) for i in range(PEER_TOPK) for j in range(PEER_TOPK) if (i + 1) * (j + 1) <= PEER_TOPK)


def _const_spec(shape):
    nd = len(shape)
    return pl.BlockSpec(shape, lambda *_: (0,) * nd, pipeline_mode=pl.Buffered(1))


def _params(n_grid):
    return pltpu.CompilerParams(dimension_semantics=("arbitrary",) * n_grid, vmem_limit_bytes=VMEM_LIMIT)


def _mod_kernel(c_ref, w_ref, b_ref, o_ref):
    c = c_ref[...]
    s = c * jax.nn.sigmoid(c)
    o_ref[...] = jnp.dot(s, w_ref[...], precision=HIGHEST, preferred_element_type=F32) + b_ref[...]


def _mod(c, w_ada, b_ada):
    rows, n = c.shape[0], w_ada.shape[1]
    tn = 512
    return pl.pallas_call(
        _mod_kernel,
        grid=(n // tn,),
        in_specs=[pl.BlockSpec((rows, D_MODEL), lambda j: (0, 0)),
                  pl.BlockSpec((D_MODEL, tn), lambda j: (0, j)),
                  pl.BlockSpec((1, tn), lambda j: (0, j))],
        out_specs=pl.BlockSpec((rows, tn), lambda j: (0, j)),
        out_shape=jax.ShapeDtypeStruct((rows, n), F32),
        compiler_params=_params(1),
        name="mod",
    )(c, w_ada, b_ada.reshape(1, n))


def _head_norm(z, g_ref, gt_ref, w):
    ss = jnp.dot(z * z, g_ref[...], precision=HIGHEST, preferred_element_type=F32)
    ssb = jnp.dot(ss, gt_ref[...], precision=HIGHEST, preferred_element_type=F32)
    return z * lax.rsqrt(ssb * (1.0 / HEAD_DIM) + EPS) * w


def _rope(x, cos, sin_signed):
    w = x.shape[1]
    lane = lax.broadcasted_iota(jnp.int32, x.shape, 1)
    fwd = pltpu.roll(x, w - HEAD_DIM // 2, axis=1)
    bwd = pltpu.roll(x, HEAD_DIM // 2, axis=1)
    partner = jnp.where((lane % HEAD_DIM) < HEAD_DIM // 2, fwd, bwd)
    reps = w // LANES
    return x * jnp.tile(cos, (1, reps)) + partner * jnp.tile(sin_signed, (1, reps))


def _in_kernel(sample, tb, steps_per_batch, *refs):
    if sample:
        (x_ref, sh_ref, sc_ref, n1_ref, w_ref, qn_ref, kn_ref, cos_ref, sin_ref, cw_ref,
         gq_ref, gqt_ref, gk_ref, gkt_ref, st_ref,
         q_ref, k_ref, v_ref, bc_ref, ga_ref, gc_ref, u_ref, prev_ref) = refs
    else:
        (x_ref, sh_ref, sc_ref, n1_ref, w_ref, qn_ref, kn_ref, cos_ref, sin_ref, cw_ref,
         gq_ref, gqt_ref, gk_ref, gkt_ref,
         q_ref, k_ref, v_ref, bc_ref, ga_ref, gc_ref, u_ref, ext_ref) = refs
    i = pl.program_id(0)
    x = x_ref[...]
    ms = jnp.mean(x * x, axis=-1, keepdims=True)
    h = (x * lax.rsqrt(ms + EPS) * n1_ref[...]) * (1.0 + sc_ref[...]) + sh_ref[...]
    hb = h.astype(BF16)

    def seg(n):
        return jnp.dot(hb, w_ref[:, IN_OFFS[n]:IN_OFFS[n + 1]], preferred_element_type=F32)

    cos, sin = cos_ref[...], sin_ref[...]
    q = _rope(_head_norm(seg(0), gq_ref, gqt_ref, qn_ref[...]), cos, sin)
    q_ref[...] = (q * (HEAD_DIM ** -0.5)).astype(BF16)
    k_ref[...] = _rope(_head_norm(seg(1), gk_ref, gkt_ref, kn_ref[...]), cos, sin)
    v_ref[...] = seg(2)
    u = seg(4) * seg(5)
    w0, w1, w2 = cw_ref[0:1, :], cw_ref[1:2, :], cw_ref[2:3, :]
    if sample:
        @pl.when(i == 0)
        def _():
            prev_ref[...] = st_ref[...]
        p2, p1 = prev_ref[0], prev_ref[1]
        conv = p2 * w0 + p1 * w1 + u * w2
        prev_ref[0] = p1
        prev_ref[1] = u
        u_ref[...] = u
    else:
        @pl.when(i % steps_per_batch == 0)
        def _():
            ext_ref[0:SUBLANES, :] = jnp.zeros((SUBLANES, D_CONV), F32)

        @pl.when(i % steps_per_batch != 0)
        def _():
            ext_ref[0:SUBLANES, :] = ext_ref[tb:tb + SUBLANES, :]
        ext_ref[SUBLANES:tb + SUBLANES, :] = u
        conv = ext_ref[SUBLANES - 2:tb + SUBLANES - 2, :] * w0 + ext_ref[SUBLANES - 1:tb + SUBLANES - 1, :] * w1 + u * w2
        u_ref[...] = u[tb - SUBLANES:tb, :]
    bc_ref[...] = (seg(3) * conv).astype(BF16)
    ga_ref[...] = jax.nn.sigmoid(seg(6))
    gc_ref[...] = jax.nn.sigmoid(seg(7))


def _group_mats(width):
    lane = np.arange(width)
    g = (lane[:, None] // HEAD_DIM == np.arange(LANES)[None, :]).astype(np.float32)
    return jnp.asarray(g), jnp.asarray(g.T)


def _in_proj(sample, x, sh, sc, norm1, w_in_bf, qn, kn, cos, sin, conv_w, state=None, *, tb, steps_per_batch):
    t = x.shape[0]
    steps = t // tb
    gq, gqt = _group_mats(Q_W)
    gk, gkt = _group_mats(KV_W)
    row = lambda w: pl.BlockSpec((tb, w), lambda i: (i, 0))
    if sample:
        mod_spec = pl.BlockSpec((tb, D_MODEL), lambda i: (0, 0))
        rope_spec = pl.BlockSpec((None, 1, LANES), lambda i: (i, 0, 0))
        u_spec, u_rows = row(D_CONV), t
        scratch = [pltpu.VMEM((2, tb, D_CONV), F32)]
    else:
        mod_spec = pl.BlockSpec((None, 1, D_MODEL), lambda i: (i // steps_per_batch, 0, 0))
        rope_spec = pl.BlockSpec((tb, LANES), lambda i: (i % steps_per_batch, 0))
        u_spec, u_rows = pl.BlockSpec((SUBLANES, D_CONV), lambda i: (i, 0)), steps * SUBLANES
        scratch = [pltpu.VMEM((tb + SUBLANES, D_CONV), F32)]
    in_specs = [row(D_MODEL), mod_spec, mod_spec, _const_spec((1, D_MODEL)), _const_spec((D_MODEL, D_IN)),
                _const_spec((1, Q_W)), _const_spec((1, KV_W)), rope_spec, rope_spec, _const_spec((CONV_WIDTH, D_CONV)),
                _const_spec(gq.shape), _const_spec(gqt.shape), _const_spec(gk.shape), _const_spec(gkt.shape)]
    args = [x, sh, sc, norm1, w_in_bf, qn, kn, cos, sin, conv_w, gq, gqt, gk, gkt]
    if sample:
        in_specs.append(_const_spec(state.shape))
        args.append(state)
    return pl.pallas_call(
        functools.partial(_in_kernel, sample, tb, steps_per_batch),
        grid=(steps,),
        in_specs=in_specs,
        out_specs=[row(Q_W), row(KV_W), row(KV_W), row(D_CONV), row(D_MODEL), row(D_MODEL), u_spec],
        out_shape=[jax.ShapeDtypeStruct((t, Q_W), BF16), jax.ShapeDtypeStruct((t, KV_W), F32),
                   jax.ShapeDtypeStruct((t, KV_W), F32), jax.ShapeDtypeStruct((t, D_CONV), BF16),
                   jax.ShapeDtypeStruct((t, D_MODEL), F32), jax.ShapeDtypeStruct((t, D_MODEL), F32),
                   jax.ShapeDtypeStruct((u_rows, D_CONV), F32)],
        scratch_shapes=scratch,
        compiler_params=_params(1),
        name="in_proj_sample" if sample else "in_proj_prompt",
    )(*args)


def _sink_softmax_pv(q, kg, vg, valid, sink):
    s = lax.dot_general(q, kg, (((1,), (1,)), ((), ())), preferred_element_type=F32)
    s = jnp.where(valid, s, NEG)
    m = jnp.maximum(jnp.max(s, axis=1, keepdims=True), sink)
    p = jnp.exp(s - m)
    denom = jnp.sum(p, axis=1, keepdims=True) + jnp.exp(sink - m)
    return jnp.dot(p.astype(BF16), vg, preferred_element_type=F32) / denom


def _attn_prompt_kernel(sink_ref, q_ref, kc_ref, kp_ref, vc_ref, vp_ref, o_ref):
    j = pl.program_id(1)
    q = q_ref[...]
    kb = jnp.concatenate([kp_ref[...], kc_ref[...]], axis=0).astype(BF16)
    vb = jnp.concatenate([vp_ref[...], vc_ref[...]], axis=0).astype(BF16)
    qi = lax.broadcasted_iota(jnp.int32, (ATTN_BLOCK, 2 * ATTN_BLOCK), 0)
    kj = lax.broadcasted_iota(jnp.int32, (ATTN_BLOCK, 2 * ATTN_BLOCK), 1)
    valid = (kj >= qi) & (kj <= qi + WINDOW) & ((kj >= ATTN_BLOCK) | (j > 0))
    outs = []
    for h in range(N_HEADS):
        g = h // GROUP
        kg = kb[:, g * HEAD_DIM:(g + 1) * HEAD_DIM]
        vg = vb[:, g * HEAD_DIM:(g + 1) * HEAD_DIM]
        outs.append(_sink_softmax_pv(q[:, h * HEAD_DIM:(h + 1) * HEAD_DIM], kg, vg, valid, sink_ref[h]))
    o_ref[...] = jnp.concatenate(outs, axis=1).astype(BF16)


def _attn_prompt(q, k, v, sinks, batch, seq):
    nb = seq // ATTN_BLOCK
    cur = lambda w: pl.BlockSpec((ATTN_BLOCK, w), lambda b, j: (b * nb + j, 0))
    prev = lambda w: pl.BlockSpec((ATTN_BLOCK, w), lambda b, j: (b * nb + jnp.maximum(j - 1, 0), 0))
    return pl.pallas_call(
        _attn_prompt_kernel,
        grid=(batch, nb),
        in_specs=[pl.BlockSpec(memory_space=pltpu.SMEM), cur(Q_W), cur(KV_W), prev(KV_W), cur(KV_W), prev(KV_W)],
        out_specs=cur(Q_W),
        out_shape=jax.ShapeDtypeStruct((batch * seq, Q_W), BF16),
        compiler_params=_params(2),
        name="attn_prompt",
    )(sinks, q, k, k, v, v)


def _attn_sample_kernel(bb, n_new, sink_ref, q_ref, kn_ref, vn_ref, kc_ref, vc_ref, o_ref):
    rows = GROUP * n_new
    s_len = 2 * WINDOW
    ri = lax.broadcasted_iota(jnp.int32, (rows, s_len), 0)
    kj = lax.broadcasted_iota(jnp.int32, (rows, s_len), 1)
    t = ri % n_new
    valid = ((kj < WINDOW) & (kj >= t)) | ((kj >= WINDOW) & (kj - WINDOW <= t))
    pad = jnp.zeros((WINDOW - SUBLANES, KV_W), F32)
    for b in range(bb):
        kb = jnp.concatenate([kc_ref[b], kn_ref[b], pad], axis=0).astype(BF16)
        vb = jnp.concatenate([vc_ref[b], vn_ref[b], pad], axis=0).astype(BF16)
        for g in range(N_KV_HEADS):
            kg = kb[:, g * HEAD_DIM:(g + 1) * HEAD_DIM]
            vg = vb[:, g * HEAD_DIM:(g + 1) * HEAD_DIM]
            o_ref[b, g] = _sink_softmax_pv(q_ref[b, g], kg, vg, valid, sink_ref[:, g:g + 1]).astype(BF16)


def _attn_sample(q4, k_new, v_new, cache_k, cache_v, sink_mat, n_new):
    nbatch = q4.shape[0]
    bb = 8 if nbatch % 8 == 0 else nbatch
    rows = GROUP * n_new
    b3 = lambda r, w: pl.BlockSpec((bb, r, w), lambda i: (i, 0, 0))
    qspec = pl.BlockSpec((bb, N_KV_HEADS, rows, HEAD_DIM), lambda i: (i, 0, 0, 0))
    return pl.pallas_call(
        functools.partial(_attn_sample_kernel, bb, n_new),
        grid=(nbatch // bb,),
        in_specs=[_const_spec(sink_mat.shape), qspec, b3(SUBLANES, KV_W), b3(SUBLANES, KV_W),
                  b3(WINDOW, KV_W), b3(WINDOW, KV_W)],
        out_specs=qspec,
        out_shape=jax.ShapeDtypeStruct(q4.shape, BF16),
        compiler_params=_params(1),
        name="attn_sample",
    )(sink_mat, q4, k_new, v_new, cache_k, cache_v)


def _topk_rows(s, k, payload=None):
    r = s.shape[0]
    rows = lax.broadcasted_iota(jnp.int32, s.shape, 0)
    vals, sel = [], []
    for _ in range(k):
        m = jnp.max(s, axis=0, keepdims=True)
        ix = jnp.min(jnp.where(s == m, rows, r), axis=0, keepdims=True)
        hit = rows == ix
        vals.append(m)
        sel.append(ix if payload is None else jnp.max(jnp.where(hit, payload, -1), axis=0, keepdims=True))
        s = jnp.where(hit, -jnp.inf, s)
    return jnp.concatenate(vals, axis=0), jnp.concatenate(sel, axis=0)


def _peer_route(q0, q1, k0, k1):
    nt = (((1,), (1,)), ((), ()))
    s0 = lax.dot_general(k0, q0, nt, preferred_element_type=F32)
    s1 = lax.dot_general(k1, q1, nt, preferred_element_type=F32)
    sv0, si0 = _topk_rows(s0, PEER_TOPK)
    sv1, si1 = _topk_rows(s1, PEER_TOPK)
    tb = s0.shape[1]
    row8 = lax.broadcasted_iota(jnp.int32, (SUBLANES, tb), 0)
    n_grp = -(-len(PEER_PAIRS) // SUBLANES)
    cvs, ces = [], []
    for gi in range(n_grp):
        cv = jnp.full((SUBLANES, tb), -jnp.inf, F32)
        ce = jnp.zeros((SUBLANES, tb), jnp.int32)
        for r in range(SUBLANES):
            c = gi * SUBLANES + r
            if c < len(PEER_PAIRS):
                i, j = PEER_PAIRS[c]
                cv = jnp.where(row8 == r, sv0[i:i + 1, :] + sv1[j:j + 1, :], cv)
                ce = jnp.where(row8 == r, si0[i:i + 1, :] * N_KEYS + si1[j:j + 1, :], ce)
        cvs.append(cv)
        ces.append(ce)
    cs, e = _topk_rows(jnp.concatenate(cvs, axis=0), PEER_TOPK, payload=jnp.concatenate(ces, axis=0))
    p = jnp.exp(cs - cs[0:1, :])
    return e, p / jnp.sum(p, axis=0, keepdims=True)


def _post_kernel(tb, x_ref, at_ref, bc_ref, ga_ref, gc_ref, gt1_ref, sh2_ref, sc2_ref, n2_ref,
                 wa_ref, wc_ref, wo_ref, wpq_ref, sk_ref, x1_ref, h2_ref, e_ref, g_ref, qp_ref):
    merged = (ga_ref[...] * jnp.dot(at_ref[...], wa_ref[...], preferred_element_type=F32)
              + gc_ref[...] * jnp.dot(bc_ref[...], wc_ref[...], preferred_element_type=F32))
    x1 = x_ref[...] + gt1_ref[...] * jnp.dot(merged.astype(BF16), wo_ref[...], preferred_element_type=F32)
    x1_ref[...] = x1
    ms = jnp.mean(x1 * x1, axis=-1, keepdims=True)
    h2 = (x1 * lax.rsqrt(ms + EPS) * n2_ref[...]) * (1.0 + sc2_ref[...]) + sh2_ref[...]
    h2_ref[...] = h2
    qp = jnp.dot(h2.astype(BF16), wpq_ref[...], preferred_element_type=F32).astype(BF16)
    for hp in range(2 * PEER_HEADS):
        qp_ref[hp] = qp[:, hp * PEER_HALF:(hp + 1) * PEER_HALF]

    def head(h, carry):
        e, g = _peer_route(qp_ref[2 * h], qp_ref[2 * h + 1], sk_ref[2 * h], sk_ref[2 * h + 1])
        e_ref[h] = e
        g_ref[h] = g
        return carry

    lax.fori_loop(0, PEER_HEADS, head, 0)


def _post(sample, x, attn, bc, ga, gc, gt1, sh2, sc2, norm2, wa, wc, wo, wpq, sk, *, tb, steps_per_batch):
    t = x.shape[0]
    row = lambda w: pl.BlockSpec((tb, w), lambda i: (i, 0))
    if sample:
        mod_spec = pl.BlockSpec((tb, D_MODEL), lambda i: (0, 0))
    else:
        mod_spec = pl.BlockSpec((None, 1, D_MODEL), lambda i: (i // steps_per_batch, 0, 0))
    sel_spec = pl.BlockSpec((PEER_HEADS, PEER_TOPK, tb), lambda i: (0, 0, i))
    return pl.pallas_call(
        functools.partial(_post_kernel, tb),
        grid=(t // tb,),
        in_specs=[row(D_MODEL), row(Q_W), row(D_CONV), row(D_MODEL), row(D_MODEL), mod_spec, mod_spec, mod_spec,
                  _const_spec((1, D_MODEL)), _const_spec(wa.shape), _const_spec(wc.shape), _const_spec(wo.shape),
                  _const_spec(wpq.shape), _const_spec(sk.shape)],
        out_specs=[row(D_MODEL), row(D_MODEL), sel_spec, sel_spec],
        out_shape=[jax.ShapeDtypeStruct((t, D_MODEL), F32), jax.ShapeDtypeStruct((t, D_MODEL), F32),
                   jax.ShapeDtypeStruct((PEER_HEADS, PEER_TOPK, t), jnp.int32),
                   jax.ShapeDtypeStruct((PEER_HEADS, PEER_TOPK, t), F32)],
        scratch_shapes=[pltpu.VMEM((2 * PEER_HEADS, tb, PEER_HALF), BF16)],
        compiler_params=_params(1),
        name="post_sample" if sample else "post_prompt",
    )(x, attn, bc, ga, gc, gt1, sh2, sc2, norm2, wa, wc, wo, wpq, sk)


def _sublane_sums(p):
    sub = lax.broadcasted_iota(jnp.int32, (SUBLANES, LANES), 0)

    def merge(a, b, sh):
        lo = (sub & sh) == 0
        fa = a + pltpu.roll(a, SUBLANES - sh, axis=0)
        fb = b + pltpu.roll(b, sh, axis=0)
        return jnp.where(lo, fa, fb)

    n = [merge(p[i], p[i + 4], 4) for i in range(4)]
    q = [merge(n[0], n[2], 2), merge(n[1], n[3], 2)]
    return merge(q[0], q[1], 1)


def _peer_kernel(rows, idx_cur, idx_nxt, h_ref, g_ref, uv_hbm, o_ref, buf, wbuf, sem):
    i = pl.program_id(0)
    n = pl.num_programs(0)
    slot = lax.rem(i, 2)

    def issue(idx_ref, s):
        def body(j, c):
            for u in range(SUBLANES):
                r = j * SUBLANES + u
                pltpu.make_async_copy(uv_hbm.at[idx_ref[r]], buf.at[s, r], sem.at[s]).start()
            return c
        lax.fori_loop(0, rows // SUBLANES, body, 0)

    @pl.when(i == 0)
    def _():
        issue(idx_cur, 0)

    @pl.when(i + 1 < n)
    def _():
        issue(idx_nxt, 1 - slot)

    pltpu.make_async_copy(uv_hbm.at[pl.ds(0, rows)], buf.at[slot], sem.at[slot]).wait()

    eye = (lax.broadcasted_iota(jnp.int32, (PEER_SEL, PEER_SEL), 0)
           == lax.broadcasted_iota(jnp.int32, (PEER_SEL, PEER_SEL), 1)).astype(F32)
    gcol = lax.dot_general(eye, g_ref[...], (((1,), (1,)), ((), ())), precision=HIGHEST, preferred_element_type=F32)
    for t in range(PEER_TOK):
        h = h_ref[t]
        base = t * PEER_SEL
        parts = []
        for k0 in range(0, PEER_SEL, SUBLANES):
            parts.append(_sublane_sums([buf[slot, base + k0 + s, 0:SUBLANES, :] * h for s in range(SUBLANES)]))
        a = jnp.sum(jnp.concatenate(parts, axis=0), axis=1, keepdims=True)
        w = gcol[:, t:t + 1] * jax.nn.gelu(a)
        wbuf[...] = jnp.broadcast_to(w, (PEER_SEL, LANES))
        accs = [jnp.zeros((SUBLANES, LANES), F32) for _ in range(4)]
        for k in range(PEER_SEL):
            accs[k % 4] = accs[k % 4] + wbuf[k:k + 1, :] * buf[slot, base + k, SUBLANES:2 * SUBLANES, :]
        o_ref[t] = (accs[0] + accs[1]) + (accs[2] + accs[3])


def _peer(idx_flat, h3, g_tok, uv3):
    t = h3.shape[0]
    n = t // PEER_TOK
    rows = PEER_TOK * PEER_SEL
    tile = pl.BlockSpec((PEER_TOK, SUBLANES, LANES), lambda i: (i, 0, 0))
    return pl.pallas_call(
        functools.partial(_peer_kernel, rows),
        grid=(n,),
        in_specs=[pl.BlockSpec((rows,), lambda i: (i,), memory_space=pltpu.SMEM),
                  pl.BlockSpec((rows,), lambda i: (jnp.minimum(i + 1, n - 1),), memory_space=pltpu.SMEM),
                  tile,
                  pl.BlockSpec((PEER_TOK, PEER_SEL), lambda i: (i, 0)),
                  pl.BlockSpec(memory_space=pl.ANY)],
        out_specs=tile,
        out_shape=jax.ShapeDtypeStruct((t, SUBLANES, LANES), F32),
        scratch_shapes=[pltpu.VMEM((2, rows, 2 * SUBLANES, LANES), F32), pltpu.VMEM((PEER_SEL, LANES), F32),
                        pltpu.SemaphoreType.DMA((2,))],
        compiler_params=pltpu.CompilerParams(dimension_semantics=("arbitrary",), vmem_limit_bytes=VMEM_LIMIT,
                                             disable_bounds_checks=True),
        name="peer_mix",
    )(idx_flat, idx_flat, h3, g_tok, uv3)


def _final_kernel(x_ref, gt_ref, o_ref, y_ref):
    y_ref[...] = x_ref[...] + gt_ref[...] * o_ref[...]


def _final(sample, x1, gt2, out, *, tb, steps_per_batch):
    t = x1.shape[0]
    row = pl.BlockSpec((tb, D_MODEL), lambda i: (i, 0))
    if sample:
        mod_spec = pl.BlockSpec((tb, D_MODEL), lambda i: (0, 0))
    else:
        mod_spec = pl.BlockSpec((None, 1, D_MODEL), lambda i: (i // steps_per_batch, 0, 0))
    return pl.pallas_call(
        _final_kernel,
        grid=(t // tb,),
        in_specs=[row, mod_spec, row],
        out_specs=row,
        out_shape=jax.ShapeDtypeStruct((t, D_MODEL), F32),
        compiler_params=_params(1),
        name="final_sample" if sample else "final_prompt",
    )(x1, gt2, out)


def _rope_tables(pos):
    half = HEAD_DIM // 2
    inv = jnp.power(jnp.float32(ROPE_THETA), -jnp.arange(half, dtype=F32) / half)
    ang = pos.astype(F32)[:, None] * inv[None, :]
    lane = np.arange(LANES)
    sign = jnp.asarray(np.where((lane % HEAD_DIM) < half, -1.0, 1.0).astype(np.float32))
    cos = jnp.cos(ang)[:, lane % half]
    sin = jnp.sin(ang)[:, lane % half] * sign[None, :]
    return cos, sin


def _layer(sample, x, mods, pos, lw, *, tb, steps_per_batch, batch, seq, win_k=None, win_v=None, conv_state=None):
    sh1, sc1, gt1, sh2, sc2, gt2 = mods
    cos, sin = _rope_tables(pos)
    if sample:
        cos, sin = cos[:, None, :], sin[:, None, :]
    q, k, v, bc, ga, gc, u_out = _in_proj(sample, x, sh1, sc1, lw["norm1"], lw["w_in"], lw["qn"], lw["kn"], cos, sin,
                                          lw["conv_w"], conv_state, tb=tb, steps_per_batch=steps_per_batch)
    if sample:
        n_new, nb = seq, batch
        q4 = q.reshape(n_new, nb, N_KV_HEADS, GROUP, HEAD_DIM).transpose(1, 2, 3, 0, 4).reshape(
            nb, N_KV_HEADS, GROUP * n_new, HEAD_DIM)
        to_b = lambda a: jnp.pad(a.reshape(n_new, nb, KV_W).transpose(1, 0, 2), ((0, 0), (0, SUBLANES - n_new), (0, 0)))
        sink_mat = jnp.repeat(lw["sinks"].reshape(N_KV_HEADS, GROUP), n_new, axis=1).T
        o4 = _attn_sample(q4, to_b(k), to_b(v), win_k.reshape(nb, WINDOW, KV_W), win_v.reshape(nb, WINDOW, KV_W),
                          sink_mat, n_new)
        attn = o4.reshape(nb, N_KV_HEADS, GROUP, n_new, HEAD_DIM).transpose(3, 0, 1, 2, 4).reshape(n_new * nb, Q_W)
    else:
        attn = _attn_prompt(q, k, v, lw["sinks"], batch, seq)
    x1, h2, e, g = _post(sample, x, attn, bc, ga, gc, gt1, sh2, sc2, lw["norm2"], lw["wa"], lw["wc"], lw["wo"],
                         lw["wpq"], lw["sk"], tb=tb, steps_per_batch=steps_per_batch)
    t = x.shape[0]
    idx = e.reshape(PEER_SEL, t).T.reshape(t * PEER_SEL)
    g_tok = g.reshape(PEER_SEL, t).T
    out = _peer(idx, h2.reshape(t, SUBLANES, LANES), g_tok, lw["uv"]).reshape(t, D_MODEL)
    y = _final(sample, x1, gt2, out, tb=tb, steps_per_batch=steps_per_batch)
    return y, k, v, u_out


def kernel(x_prompt, x_sample, c_prompt, c_sample, cache_win_k, cache_win_v, state_conv, w_ada, b_ada, norm1, w_in, q_norm, k_norm, attn_sinks, conv_w, w_attn_o, w_conv_o, w_out, norm2, w_peer_q, peer_sub_keys, peer_u, peer_v):
    depth = w_ada.shape[0]
    bp, sp, _ = x_prompt.shape
    bs, ss, _ = x_sample.shape
    assert sp % ATTN_BLOCK == 0 and ss <= SUBLANES and (bs * ss) % PEER_TOK == 0
    tb_p = 256 if sp % 256 == 0 else ATTN_BLOCK
    pos_p = jnp.arange(sp)
    pos_s = PAST_LEN + jnp.arange(ss)
    yp = x_prompt.reshape(bp * sp, D_MODEL)
    ys = x_sample.transpose(1, 0, 2).reshape(ss * bs, D_MODEL)
    n_experts = peer_u.shape[1]
    kp_l, vp_l, cp_l, ks_l, vs_l, cs_l = [], [], [], [], [], []
    for l in range(depth):
        lw = {
            "norm1": norm1[l].reshape(1, D_MODEL), "norm2": norm2[l].reshape(1, D_MODEL),
            "w_in": w_in[l].astype(BF16),
            "qn": jnp.tile(q_norm[l], N_HEADS).reshape(1, Q_W), "kn": jnp.tile(k_norm[l], N_KV_HEADS).reshape(1, KV_W),
            "sinks": attn_sinks[l], "conv_w": conv_w[l],
            "wa": w_attn_o[l].astype(BF16), "wc": w_conv_o[l].astype(BF16), "wo": w_out[l].astype(BF16),
            "wpq": w_peer_q[l].astype(BF16),
            "sk": peer_sub_keys[l].reshape(2 * PEER_HEADS, N_KEYS, PEER_HALF).astype(BF16),
            "uv": jnp.concatenate([peer_u[l].reshape(n_experts, SUBLANES, LANES),
                                   peer_v[l].reshape(n_experts, SUBLANES, LANES)], axis=1),
        }
        c_all = jnp.concatenate([c_prompt, c_sample], axis=0)
        pad = (-c_all.shape[0]) % SUBLANES
        mod = _mod(jnp.pad(c_all, ((0, pad), (0, 0))), w_ada[l], b_ada[l])
        mods_p = tuple(m.reshape(bp, 1, D_MODEL) for m in jnp.split(mod[:bp], 6, axis=-1))
        mods_s = tuple(jnp.split(mod[bp:bp + bs], 6, axis=-1))
        yp, kp, vp, up = _layer(False, yp, mods_p, pos_p, lw, tb=tb_p, steps_per_batch=sp // tb_p, batch=bp, seq=sp)
        ys, ksn, vsn, us = _layer(True, ys, mods_s, pos_s, lw, tb=bs, steps_per_batch=1, batch=bs, seq=ss,
                                  win_k=cache_win_k[l], win_v=cache_win_v[l],
                                  conv_state=state_conv[l].transpose(1, 0, 2))
        kp_l.append(kp.reshape(bp, sp, N_KV_HEADS, HEAD_DIM)[:, -WINDOW:])
        vp_l.append(vp.reshape(bp, sp, N_KV_HEADS, HEAD_DIM)[:, -WINDOW:])
        cp_l.append(up.reshape(bp, sp // tb_p, SUBLANES, D_CONV)[:, -1, -(CONV_WIDTH - 1):])
        k_new = ksn.reshape(ss, bs, N_KV_HEADS, HEAD_DIM).transpose(1, 0, 2, 3)
        v_new = vsn.reshape(ss, bs, N_KV_HEADS, HEAD_DIM).transpose(1, 0, 2, 3)
        ks_l.append(jnp.concatenate([cache_win_k[l], k_new], axis=1)[:, -WINDOW:])
        vs_l.append(jnp.concatenate([cache_win_v[l], v_new], axis=1)[:, -WINDOW:])
        ext = jnp.concatenate([state_conv[l], us.reshape(ss, bs, D_CONV).transpose(1, 0, 2)], axis=1)
        cs_l.append(ext[:, -(CONV_WIDTH - 1):])
    return (yp.reshape(bp, sp, D_MODEL), ys.reshape(ss, bs, D_MODEL).transpose(1, 0, 2),
            jnp.stack(kp_l), jnp.stack(vp_l), jnp.stack(cp_l), jnp.stack(ks_l), jnp.stack(vs_l), jnp.stack(cs_l))
```

```python
import functools

import numpy as np
import jax
import jax.numpy as jnp
from jax import lax
from jax.experimental import pallas as pl
from jax.experimental.pallas import tpu as pltpu

F32 = jnp.float32
BF16 = jnp.bfloat16
HIGHEST = lax.Precision.HIGHEST

D_MODEL = 1024
N_HEADS = 16
HEAD_DIM = 64
N_KV_HEADS = 4
GROUP = N_HEADS // N_KV_HEADS
WINDOW = 128
ATTN_BLOCK = 128
ROPE_THETA = 10000.0
Q_W = N_HEADS * HEAD_DIM
KV_W = N_KV_HEADS * HEAD_DIM
D_CONV = 1024
CONV_WIDTH = 3
PEER_HEADS = 8
N_KEYS = 128
PEER_TOPK = 16
PEER_HALF = 128
PEER_SEL = PEER_HEADS * PEER_TOPK
PAST_LEN = 16384
EPS = 1e-6
NEG = -1e30
IN_SIZES = (Q_W, KV_W, KV_W, D_CONV, D_CONV, D_CONV, D_MODEL, D_MODEL)
IN_OFFS = tuple(int(v) for v in np.cumsum((0,) + IN_SIZES))
D_IN = IN_OFFS[-1]

LANES = 128
SUBLANES = 8
VMEM_LIMIT = 56 * 1024 * 1024
PEER_TOK = 8
PEER_DMA_QUEUES = 2
PEER_PAIRS = tuple((i, j) for i in range(PEER_TOPK) for j in range(PEER_TOPK) if (i + 1) * (j + 1) <= PEER_TOPK)


def _const_spec(shape):
    nd = len(shape)
    return pl.BlockSpec(shape, lambda *_: (0,) * nd, pipeline_mode=pl.Buffered(1))


def _params(n_grid):
    return pltpu.CompilerParams(dimension_semantics=("arbitrary",) * n_grid, vmem_limit_bytes=VMEM_LIMIT)


def _mod_kernel(c_ref, w_ref, b_ref, o_ref):
    c = c_ref[...]
    s = c * jax.nn.sigmoid(c)
    o_ref[...] = jnp.dot(s, w_ref[...], precision=HIGHEST, preferred_element_type=F32) + b_ref[...]


def _mod(c, w_ada, b_ada):
    rows, n = c.shape[0], w_ada.shape[1]
    tn = 512
    return pl.pallas_call(
        _mod_kernel,
        grid=(n // tn,),
        in_specs=[pl.BlockSpec((rows, D_MODEL), lambda j: (0, 0)),
                  pl.BlockSpec((D_MODEL, tn), lambda j: (0, j)),
                  pl.BlockSpec((1, tn), lambda j: (0, j))],
        out_specs=pl.BlockSpec((rows, tn), lambda j: (0, j)),
        out_shape=jax.ShapeDtypeStruct((rows, n), F32),
        compiler_params=_params(1),
        name="mod",
    )(c, w_ada, b_ada.reshape(1, n))


def _head_norm(z, g_ref, gt_ref, w):
    ss = jnp.dot(z * z, g_ref[...], precision=HIGHEST, preferred_element_type=F32)
    ssb = jnp.dot(ss, gt_ref[...], precision=HIGHEST, preferred_element_type=F32)
    return z * lax.rsqrt(ssb * (1.0 / HEAD_DIM) + EPS) * w


def _rope(x, cos, sin_signed):
    w = x.shape[1]
    lane = lax.broadcasted_iota(jnp.int32, x.shape, 1)
    fwd = pltpu.roll(x, w - HEAD_DIM // 2, axis=1)
    bwd = pltpu.roll(x, HEAD_DIM // 2, axis=1)
    partner = jnp.where((lane % HEAD_DIM) < HEAD_DIM // 2, fwd, bwd)
    reps = w // LANES
    return x * jnp.tile(cos, (1, reps)) + partner * jnp.tile(sin_signed, (1, reps))


def _in_kernel(sample, tb, steps_per_batch, *refs):
    if sample:
        (x_ref, sh_ref, sc_ref, n1_ref, w_ref, qn_ref, kn_ref, cos_ref, sin_ref, cw_ref,
         gq_ref, gqt_ref, gk_ref, gkt_ref, st_ref,
         q_ref, k_ref, v_ref, bc_ref, ga_ref, gc_ref, u_ref, prev_ref) = refs
    else:
        (x_ref, sh_ref, sc_ref, n1_ref, w_ref, qn_ref, kn_ref, cos_ref, sin_ref, cw_ref,
         gq_ref, gqt_ref, gk_ref, gkt_ref,
         q_ref, k_ref, v_ref, bc_ref, ga_ref, gc_ref, u_ref, ext_ref) = refs
    i = pl.program_id(0)
    x = x_ref[...]
    ms = jnp.mean(x * x, axis=-1, keepdims=True)
    h = (x * lax.rsqrt(ms + EPS) * n1_ref[...]) * (1.0 + sc_ref[...]) + sh_ref[...]
    hb = h.astype(BF16)

    def seg(n):
        return jnp.dot(hb, w_ref[:, IN_OFFS[n]:IN_OFFS[n + 1]], preferred_element_type=F32)

    cos, sin = cos_ref[...], sin_ref[...]
    q = _rope(_head_norm(seg(0), gq_ref, gqt_ref, qn_ref[...]), cos, sin)
    q_ref[...] = (q * (HEAD_DIM ** -0.5)).astype(BF16)
    k_ref[...] = _rope(_head_norm(seg(1), gk_ref, gkt_ref, kn_ref[...]), cos, sin)
    v_ref[...] = seg(2)
    u = seg(4) * seg(5)
    w0, w1, w2 = cw_ref[0:1, :], cw_ref[1:2, :], cw_ref[2:3, :]
    if sample:
        @pl.when(i == 0)
        def _():
            prev_ref[...] = st_ref[...]
        p2, p1 = prev_ref[0], prev_ref[1]
        conv = p2 * w0 + p1 * w1 + u * w2
        prev_ref[0] = p1
        prev_ref[1] = u
        u_ref[...] = u
    else:
        @pl.when(i % steps_per_batch == 0)
        def _():
            ext_ref[0:SUBLANES, :] = jnp.zeros((SUBLANES, D_CONV), F32)

        @pl.when(i % steps_per_batch != 0)
        def _():
            ext_ref[0:SUBLANES, :] = ext_ref[tb:tb + SUBLANES, :]
        ext_ref[SUBLANES:tb + SUBLANES, :] = u
        conv = ext_ref[SUBLANES - 2:tb + SUBLANES - 2, :] * w0 + ext_ref[SUBLANES - 1:tb + SUBLANES - 1, :] * w1 + u * w2
        u_ref[...] = u[tb - SUBLANES:tb, :]
    bc_ref[...] = (seg(3) * conv).astype(BF16)
    ga_ref[...] = jax.nn.sigmoid(seg(6))
    gc_ref[...] = jax.nn.sigmoid(seg(7))


def _group_mats(width):
    lane = np.arange(width)
    g = (lane[:, None] // HEAD_DIM == np.arange(LANES)[None, :]).astype(np.float32)
    return jnp.asarray(g), jnp.asarray(g.T)


def _in_proj(sample, x, sh, sc, norm1, w_in_bf, qn, kn, cos, sin, conv_w, state=None, *, tb, steps_per_batch):
    t = x.shape[0]
    steps = t // tb
    gq, gqt = _group_mats(Q_W)
    gk, gkt = _group_mats(KV_W)
    row = lambda w: pl.BlockSpec((tb, w), lambda i: (i, 0))
    if sample:
        mod_spec = pl.BlockSpec((tb, D_MODEL), lambda i: (0, 0))
        rope_spec = pl.BlockSpec((None, 1, LANES), lambda i: (i, 0, 0))
        u_spec, u_rows = row(D_CONV), t
        scratch = [pltpu.VMEM((2, tb, D_CONV), F32)]
    else:
        mod_spec = pl.BlockSpec((None, 1, D_MODEL), lambda i: (i // steps_per_batch, 0, 0))
        rope_spec = pl.BlockSpec((tb, LANES), lambda i: (i % steps_per_batch, 0))
        u_spec, u_rows = pl.BlockSpec((SUBLANES, D_CONV), lambda i: (i, 0)), steps * SUBLANES
        scratch = [pltpu.VMEM((tb + SUBLANES, D_CONV), F32)]
    in_specs = [row(D_MODEL), mod_spec, mod_spec, _const_spec((1, D_MODEL)), _const_spec((D_MODEL, D_IN)),
                _const_spec((1, Q_W)), _const_spec((1, KV_W)), rope_spec, rope_spec, _const_spec((CONV_WIDTH, D_CONV)),
                _const_spec(gq.shape), _const_spec(gqt.shape), _const_spec(gk.shape), _const_spec(gkt.shape)]
    args = [x, sh, sc, norm1, w_in_bf, qn, kn, cos, sin, conv_w, gq, gqt, gk, gkt]
    if sample:
        in_specs.append(_const_spec(state.shape))
        args.append(state)
    return pl.pallas_call(
        functools.partial(_in_kernel, sample, tb, steps_per_batch),
        grid=(steps,),
        in_specs=in_specs,
        out_specs=[row(Q_W), row(KV_W), row(KV_W), row(D_CONV), row(D_MODEL), row(D_MODEL), u_spec],
        out_shape=[jax.ShapeDtypeStruct((t, Q_W), BF16), jax.ShapeDtypeStruct((t, KV_W), F32),
                   jax.ShapeDtypeStruct((t, KV_W), F32), jax.ShapeDtypeStruct((t, D_CONV), BF16),
                   jax.ShapeDtypeStruct((t, D_MODEL), F32), jax.ShapeDtypeStruct((t, D_MODEL), F32),
                   jax.ShapeDtypeStruct((u_rows, D_CONV), F32)],
        scratch_shapes=scratch,
        compiler_params=_params(1),
        name="in_proj_sample" if sample else "in_proj_prompt",
    )(*args)


def _sink_softmax_pv(q, kg, vg, valid, sink):
    s = lax.dot_general(q, kg, (((1,), (1,)), ((), ())), preferred_element_type=F32)
    s = jnp.where(valid, s, NEG)
    m = jnp.maximum(jnp.max(s, axis=1, keepdims=True), sink)
    p = jnp.exp(s - m)
    denom = jnp.sum(p, axis=1, keepdims=True) + jnp.exp(sink - m)
    return jnp.dot(p.astype(BF16), vg, preferred_element_type=F32) / denom


def _attn_prompt_kernel(sink_ref, q_ref, kc_ref, kp_ref, vc_ref, vp_ref, o_ref):
    j = pl.program_id(1)
    q = q_ref[...]
    kb = jnp.concatenate([kp_ref[...], kc_ref[...]], axis=0).astype(BF16)
    vb = jnp.concatenate([vp_ref[...], vc_ref[...]], axis=0).astype(BF16)
    qi = lax.broadcasted_iota(jnp.int32, (ATTN_BLOCK, 2 * ATTN_BLOCK), 0)
    kj = lax.broadcasted_iota(jnp.int32, (ATTN_BLOCK, 2 * ATTN_BLOCK), 1)
    valid = (kj >= qi) & (kj <= qi + WINDOW) & ((kj >= ATTN_BLOCK) | (j > 0))
    outs = []
    for h in range(N_HEADS):
        g = h // GROUP
        kg = kb[:, g * HEAD_DIM:(g + 1) * HEAD_DIM]
        vg = vb[:, g * HEAD_DIM:(g + 1) * HEAD_DIM]
        outs.append(_sink_softmax_pv(q[:, h * HEAD_DIM:(h + 1) * HEAD_DIM], kg, vg, valid, sink_ref[h]))
    o_ref[...] = jnp.concatenate(outs, axis=1).astype(BF16)


def _attn_prompt(q, k, v, sinks, batch, seq):
    nb = seq // ATTN_BLOCK
    cur = lambda w: pl.BlockSpec((ATTN_BLOCK, w), lambda b, j: (b * nb + j, 0))
    prev = lambda w: pl.BlockSpec((ATTN_BLOCK, w), lambda b, j: (b * nb + jnp.maximum(j - 1, 0), 0))
    return pl.pallas_call(
        _attn_prompt_kernel,
        grid=(batch, nb),
        in_specs=[pl.BlockSpec(memory_space=pltpu.SMEM), cur(Q_W), cur(KV_W), prev(KV_W), cur(KV_W), prev(KV_W)],
        out_specs=cur(Q_W),
        out_shape=jax.ShapeDtypeStruct((batch * seq, Q_W), BF16),
        compiler_params=_params(2),
        name="attn_prompt",
    )(sinks, q, k, k, v, v)


def _attn_sample_kernel(bb, n_new, sink_ref, q_ref, kn_ref, vn_ref, kc_ref, vc_ref, o_ref):
    rows = GROUP * n_new
    s_len = 2 * WINDOW
    ri = lax.broadcasted_iota(jnp.int32, (rows, s_len), 0)
    kj = lax.broadcasted_iota(jnp.int32, (rows, s_len), 1)
    t = ri % n_new
    valid = ((kj < WINDOW) & (kj >= t)) | ((kj >= WINDOW) & (kj - WINDOW <= t))
    pad = jnp.zeros((WINDOW - SUBLANES, KV_W), F32)
    for b in range(bb):
        kb = jnp.concatenate([kc_ref[b], kn_ref[b], pad], axis=0).astype(BF16)
        vb = jnp.concatenate([vc_ref[b], vn_ref[b], pad], axis=0).astype(BF16)
        for g in range(N_KV_HEADS):
            kg = kb[:, g * HEAD_DIM:(g + 1) * HEAD_DIM]
            vg = vb[:, g * HEAD_DIM:(g + 1) * HEAD_DIM]
            o_ref[b, g] = _sink_softmax_pv(q_ref[b, g], kg, vg, valid, sink_ref[:, g:g + 1]).astype(BF16)


def _attn_sample(q4, k_new, v_new, cache_k, cache_v, sink_mat, n_new):
    nbatch = q4.shape[0]
    bb = 8 if nbatch % 8 == 0 else nbatch
    rows = GROUP * n_new
    b3 = lambda r, w: pl.BlockSpec((bb, r, w), lambda i: (i, 0, 0))
    qspec = pl.BlockSpec((bb, N_KV_HEADS, rows, HEAD_DIM), lambda i: (i, 0, 0, 0))
    return pl.pallas_call(
        functools.partial(_attn_sample_kernel, bb, n_new),
        grid=(nbatch // bb,),
        in_specs=[_const_spec(sink_mat.shape), qspec, b3(SUBLANES, KV_W), b3(SUBLANES, KV_W),
                  b3(WINDOW, KV_W), b3(WINDOW, KV_W)],
        out_specs=qspec,
        out_shape=jax.ShapeDtypeStruct(q4.shape, BF16),
        compiler_params=_params(1),
        name="attn_sample",
    )(sink_mat, q4, k_new, v_new, cache_k, cache_v)


def _topk_rows(s, k, payload=None):
    r = s.shape[0]
    rows = lax.broadcasted_iota(jnp.int32, s.shape, 0)
    vals, sel = [], []
    for _ in range(k):
        m = jnp.max(s, axis=0, keepdims=True)
        ix = jnp.min(jnp.where(s == m, rows, r), axis=0, keepdims=True)
        hit = rows == ix
        vals.append(m)
        sel.append(ix if payload is None else jnp.max(jnp.where(hit, payload, -1), axis=0, keepdims=True))
        s = jnp.where(hit, -jnp.inf, s)
    return jnp.concatenate(vals, axis=0), jnp.concatenate(sel, axis=0)


def _peer_route(q0, q1, k0, k1):
    nt = (((1,), (1,)), ((), ()))
    s0 = lax.dot_general(k0, q0, nt, preferred_element_type=F32)
    s1 = lax.dot_general(k1, q1, nt, preferred_element_type=F32)
    sv0, si0 = _topk_rows(s0, PEER_TOPK)
    sv1, si1 = _topk_rows(s1, PEER_TOPK)
    tb = s0.shape[1]
    row8 = lax.broadcasted_iota(jnp.int32, (SUBLANES, tb), 0)
    n_grp = -(-len(PEER_PAIRS) // SUBLANES)
    cvs, ces = [], []
    for gi in range(n_grp):
        cv = jnp.full((SUBLANES, tb), -jnp.inf, F32)
        ce = jnp.zeros((SUBLANES, tb), jnp.int32)
        for r in range(SUBLANES):
            c = gi * SUBLANES + r
            if c < len(PEER_PAIRS):
                i, j = PEER_PAIRS[c]
                cv = jnp.where(row8 == r, sv0[i:i + 1, :] + sv1[j:j + 1, :], cv)
                ce = jnp.where(row8 == r, si0[i:i + 1, :] * N_KEYS + si1[j:j + 1, :], ce)
        cvs.append(cv)
        ces.append(ce)
    cs, e = _topk_rows(jnp.concatenate(cvs, axis=0), PEER_TOPK, payload=jnp.concatenate(ces, axis=0))
    p = jnp.exp(cs - cs[0:1, :])
    return e, p / jnp.sum(p, axis=0, keepdims=True)


def _post_kernel(tb, x_ref, at_ref, bc_ref, ga_ref, gc_ref, gt1_ref, sh2_ref, sc2_ref, n2_ref,
                 wa_ref, wc_ref, wo_ref, wpq_ref, sk_ref, x1_ref, h2_ref, e_ref, g_ref, qp_ref):
    merged = (ga_ref[...] * jnp.dot(at_ref[...], wa_ref[...], preferred_element_type=F32)
              + gc_ref[...] * jnp.dot(bc_ref[...], wc_ref[...], preferred_element_type=F32))
    x1 = x_ref[...] + gt1_ref[...] * jnp.dot(merged.astype(BF16), wo_ref[...], preferred_element_type=F32)
    x1_ref[...] = x1
    ms = jnp.mean(x1 * x1, axis=-1, keepdims=True)
    h2 = (x1 * lax.rsqrt(ms + EPS) * n2_ref[...]) * (1.0 + sc2_ref[...]) + sh2_ref[...]
    h2_ref[...] = h2
    qp = jnp.dot(h2.astype(BF16), wpq_ref[...], preferred_element_type=F32).astype(BF16)
    for hp in range(2 * PEER_HEADS):
        qp_ref[hp] = qp[:, hp * PEER_HALF:(hp + 1) * PEER_HALF]

    def head(h, carry):
        e, g = _peer_route(qp_ref[2 * h], qp_ref[2 * h + 1], sk_ref[2 * h], sk_ref[2 * h + 1])
        e_ref[h] = e
        g_ref[h] = g
        return carry

    lax.fori_loop(0, PEER_HEADS, head, 0)


def _post(sample, x, attn, bc, ga, gc, gt1, sh2, sc2, norm2, wa, wc, wo, wpq, sk, *, tb, steps_per_batch):
    t = x.shape[0]
    row = lambda w: pl.BlockSpec((tb, w), lambda i: (i, 0))
    if sample:
        mod_spec = pl.BlockSpec((tb, D_MODEL), lambda i: (0, 0))
    else:
        mod_spec = pl.BlockSpec((None, 1, D_MODEL), lambda i: (i // steps_per_batch, 0, 0))
    sel_spec = pl.BlockSpec((PEER_HEADS, PEER_TOPK, tb), lambda i: (0, 0, i))
    return pl.pallas_call(
        functools.partial(_post_kernel, tb),
        grid=(t // tb,),
        in_specs=[row(D_MODEL), row(Q_W), row(D_CONV), row(D_MODEL), row(D_MODEL), mod_spec, mod_spec, mod_spec,
                  _const_spec((1, D_MODEL)), _const_spec(wa.shape), _const_spec(wc.shape), _const_spec(wo.shape),
                  _const_spec(wpq.shape), _const_spec(sk.shape)],
        out_specs=[row(D_MODEL), row(D_MODEL), sel_spec, sel_spec],
        out_shape=[jax.ShapeDtypeStruct((t, D_MODEL), F32), jax.ShapeDtypeStruct((t, D_MODEL), F32),
                   jax.ShapeDtypeStruct((PEER_HEADS, PEER_TOPK, t), jnp.int32),
                   jax.ShapeDtypeStruct((PEER_HEADS, PEER_TOPK, t), F32)],
        scratch_shapes=[pltpu.VMEM((2 * PEER_HEADS, tb, PEER_HALF), BF16)],
        compiler_params=_params(1),
        name="post_sample" if sample else "post_prompt",
    )(x, attn, bc, ga, gc, gt1, sh2, sc2, norm2, wa, wc, wo, wpq, sk)


def _sublane_sums(p):
    sub = lax.broadcasted_iota(jnp.int32, (SUBLANES, LANES), 0)

    def merge(a, b, sh):
        lo = (sub & sh) == 0
        fa = a + pltpu.roll(a, SUBLANES - sh, axis=0)
        fb = b + pltpu.roll(b, sh, axis=0)
        return jnp.where(lo, fa, fb)

    n = [merge(p[i], p[i + 4], 4) for i in range(4)]
    q = [merge(n[0], n[2], 2), merge(n[1], n[3], 2)]
    return merge(q[0], q[1], 1)


def _peer_kernel(n, idx_cur, idx_nxt, h_ref, g_ref, uv_hbm, o_ref, buf0, buf1, wbuf, sem):
    i = pl.program_id(0)
    phase_rows = PEER_TOK * PEER_SEL
    bufs = (buf0, buf1)

    def token_rows(s, t):
        return bufs[s].at[pl.ds(t * PEER_SEL, PEER_SEL)]

    def issue_token(idx_ref, off, s, t):
        for k in range(PEER_SEL):
            r = t * PEER_SEL + k
            pltpu.make_async_copy(uv_hbm.at[idx_ref[off + r]], bufs[s].at[r], sem.at[s, t]).start(
                priority=k % PEER_DMA_QUEUES)

    def wait_token(s, t):
        pltpu.make_async_copy(uv_hbm.at[pl.ds(0, PEER_SEL)], token_rows(s, t), sem.at[s, t]).wait()

    @pl.when(i == 0)
    def _():
        def body(r, c):
            pltpu.make_async_copy(uv_hbm.at[idx_cur[r]], buf0.at[r], sem.at[0, r // PEER_SEL]).start()
            return c
        lax.fori_loop(0, phase_rows, body, 0)

    eye = (lax.broadcasted_iota(jnp.int32, (PEER_SEL, PEER_SEL), 0)
           == lax.broadcasted_iota(jnp.int32, (PEER_SEL, PEER_SEL), 1)).astype(F32)
    gcol = lax.dot_general(eye, g_ref[...], (((1,), (1,)), ((), ())), precision=HIGHEST, preferred_element_type=F32)
    for s in range(2):
        for t in range(PEER_TOK):
            wait_token(s, t)
            if s == 0:
                issue_token(idx_cur, phase_rows, 1, t)
            else:
                issue_token(idx_nxt, 0, 0, t)
            tok = s * PEER_TOK + t
            h = h_ref[tok]
            base = t * PEER_SEL
            parts = []
            for k0 in range(0, PEER_SEL, SUBLANES):
                parts.append(_sublane_sums([bufs[s][base + k0 + j, 0:SUBLANES, :] * h for j in range(SUBLANES)]))
            a = jnp.sum(jnp.concatenate(parts, axis=0), axis=1, keepdims=True)
            w = gcol[:, tok:tok + 1] * jax.nn.gelu(a)
            wbuf[...] = jnp.broadcast_to(w, (PEER_SEL, LANES))
            accs = [jnp.zeros((SUBLANES, LANES), F32) for _ in range(4)]
            for k in range(PEER_SEL):
                accs[k % 4] = accs[k % 4] + wbuf[k:k + 1, :] * bufs[s][base + k, SUBLANES:2 * SUBLANES, :]
            o_ref[tok] = (accs[0] + accs[1]) + (accs[2] + accs[3])

    @pl.when(i == n - 1)
    def _():
        for t in range(PEER_TOK):
            wait_token(0, t)


def _peer(idx_flat, h3, g_tok, uv3):
    t = h3.shape[0]
    step_tok = 2 * PEER_TOK
    n = t // step_tok
    rows = step_tok * PEER_SEL
    tile = pl.BlockSpec((step_tok, SUBLANES, LANES), lambda i: (i, 0, 0))
    return pl.pallas_call(
        functools.partial(_peer_kernel, n),
        grid=(n,),
        in_specs=[pl.BlockSpec((rows,), lambda i: (i,), memory_space=pltpu.SMEM),
                  pl.BlockSpec((rows,), lambda i: (jnp.minimum(i + 1, n - 1),), memory_space=pltpu.SMEM),
                  tile,
                  pl.BlockSpec((step_tok, PEER_SEL), lambda i: (i, 0)),
                  pl.BlockSpec(memory_space=pl.ANY)],
        out_specs=tile,
        out_shape=jax.ShapeDtypeStruct((t, SUBLANES, LANES), F32),
        scratch_shapes=[pltpu.VMEM((PEER_TOK * PEER_SEL, 2 * SUBLANES, LANES), F32),
                        pltpu.VMEM((PEER_TOK * PEER_SEL, 2 * SUBLANES, LANES), F32),
                        pltpu.VMEM((PEER_SEL, LANES), F32),
                        pltpu.SemaphoreType.DMA((2, PEER_TOK))],
        compiler_params=pltpu.CompilerParams(dimension_semantics=("arbitrary",), vmem_limit_bytes=VMEM_LIMIT,
                                             disable_bounds_checks=True),
        name="peer_mix",
    )(idx_flat, idx_flat, h3, g_tok, uv3)


def _final_kernel(x_ref, gt_ref, o_ref, y_ref):
    y_ref[...] = x_ref[...] + gt_ref[...] * o_ref[...]


def _final(sample, x1, gt2, out, *, tb, steps_per_batch):
    t = x1.shape[0]
    row = pl.BlockSpec((tb, D_MODEL), lambda i: (i, 0))
    if sample:
        mod_spec = pl.BlockSpec((tb, D_MODEL), lambda i: (0, 0))
    else:
        mod_spec = pl.BlockSpec((None, 1, D_MODEL), lambda i: (i // steps_per_batch, 0, 0))
    return pl.pallas_call(
        _final_kernel,
        grid=(t // tb,),
        in_specs=[row, mod_spec, row],
        out_specs=row,
        out_shape=jax.ShapeDtypeStruct((t, D_MODEL), F32),
        compiler_params=_params(1),
        name="final_sample" if sample else "final_prompt",
    )(x1, gt2, out)


def _rope_tables(pos):
    half = HEAD_DIM // 2
    inv = jnp.power(jnp.float32(ROPE_THETA), -jnp.arange(half, dtype=F32) / half)
    ang = pos.astype(F32)[:, None] * inv[None, :]
    lane = np.arange(LANES)
    sign = jnp.asarray(np.where((lane % HEAD_DIM) < half, -1.0, 1.0).astype(np.float32))
    cos = jnp.cos(ang)[:, lane % half]
    sin = jnp.sin(ang)[:, lane % half] * sign[None, :]
    return cos, sin


def _layer(sample, x, mods, pos, lw, *, tb, steps_per_batch, batch, seq, win_k=None, win_v=None, conv_state=None):
    sh1, sc1, gt1, sh2, sc2, gt2 = mods
    cos, sin = _rope_tables(pos)
    if sample:
        cos, sin = cos[:, None, :], sin[:, None, :]
    q, k, v, bc, ga, gc, u_out = _in_proj(sample, x, sh1, sc1, lw["norm1"], lw["w_in"], lw["qn"], lw["kn"], cos, sin,
                                          lw["conv_w"], conv_state, tb=tb, steps_per_batch=steps_per_batch)
    if sample:
        n_new, nb = seq, batch
        q4 = q.reshape(n_new, nb, N_KV_HEADS, GROUP, HEAD_DIM).transpose(1, 2, 3, 0, 4).reshape(
            nb, N_KV_HEADS, GROUP * n_new, HEAD_DIM)
        to_b = lambda a: jnp.pad(a.reshape(n_new, nb, KV_W).transpose(1, 0, 2), ((0, 0), (0, SUBLANES - n_new), (0, 0)))
        sink_mat = jnp.repeat(lw["sinks"].reshape(N_KV_HEADS, GROUP), n_new, axis=1).T
        o4 = _attn_sample(q4, to_b(k), to_b(v), win_k.reshape(nb, WINDOW, KV_W), win_v.reshape(nb, WINDOW, KV_W),
                          sink_mat, n_new)
        attn = o4.reshape(nb, N_KV_HEADS, GROUP, n_new, HEAD_DIM).transpose(3, 0, 1, 2, 4).reshape(n_new * nb, Q_W)
    else:
        attn = _attn_prompt(q, k, v, lw["sinks"], batch, seq)
    x1, h2, e, g = _post(sample, x, attn, bc, ga, gc, gt1, sh2, sc2, lw["norm2"], lw["wa"], lw["wc"], lw["wo"],
                         lw["wpq"], lw["sk"], tb=tb, steps_per_batch=steps_per_batch)
    t = x.shape[0]
    idx = e.reshape(PEER_SEL, t).T.reshape(t * PEER_SEL)
    g_tok = g.reshape(PEER_SEL, t).T
    out = _peer(idx, h2.reshape(t, SUBLANES, LANES), g_tok, lw["uv"]).reshape(t, D_MODEL)
    y = _final(sample, x1, gt2, out, tb=tb, steps_per_batch=steps_per_batch)
    return y, k, v, u_out


def kernel(x_prompt, x_sample, c_prompt, c_sample, cache_win_k, cache_win_v, state_conv, w_ada, b_ada, norm1, w_in, q_norm, k_norm, attn_sinks, conv_w, w_attn_o, w_conv_o, w_out, norm2, w_peer_q, peer_sub_keys, peer_u, peer_v):
    depth = w_ada.shape[0]
    bp, sp, _ = x_prompt.shape
    bs, ss, _ = x_sample.shape
    assert sp % ATTN_BLOCK == 0 and ss <= SUBLANES and (bs * ss) % (2 * PEER_TOK) == 0
    tb_p = 256 if sp % 256 == 0 else ATTN_BLOCK
    pos_p = jnp.arange(sp)
    pos_s = PAST_LEN + jnp.arange(ss)
    yp = x_prompt.reshape(bp * sp, D_MODEL)
    ys = x_sample.transpose(1, 0, 2).reshape(ss * bs, D_MODEL)
    n_experts = peer_u.shape[1]
    kp_l, vp_l, cp_l, ks_l, vs_l, cs_l = [], [], [], [], [], []
    for l in range(depth):
        lw = {
            "norm1": norm1[l].reshape(1, D_MODEL), "norm2": norm2[l].reshape(1, D_MODEL),
            "w_in": w_in[l].astype(BF16),
            "qn": jnp.tile(q_norm[l], N_HEADS).reshape(1, Q_W), "kn": jnp.tile(k_norm[l], N_KV_HEADS).reshape(1, KV_W),
            "sinks": attn_sinks[l], "conv_w": conv_w[l],
            "wa": w_attn_o[l].astype(BF16), "wc": w_conv_o[l].astype(BF16), "wo": w_out[l].astype(BF16),
            "wpq": w_peer_q[l].astype(BF16),
            "sk": peer_sub_keys[l].reshape(2 * PEER_HEADS, N_KEYS, PEER_HALF).astype(BF16),
            "uv": jnp.concatenate([peer_u[l].reshape(n_experts, SUBLANES, LANES),
                                   peer_v[l].reshape(n_experts, SUBLANES, LANES)], axis=1),
        }
        c_all = jnp.concatenate([c_prompt, c_sample], axis=0)
        pad = (-c_all.shape[0]) % SUBLANES
        mod = _mod(jnp.pad(c_all, ((0, pad), (0, 0))), w_ada[l], b_ada[l])
        mods_p = tuple(m.reshape(bp, 1, D_MODEL) for m in jnp.split(mod[:bp], 6, axis=-1))
        mods_s = tuple(jnp.split(mod[bp:bp + bs], 6, axis=-1))
        yp, kp, vp, up = _layer(False, yp, mods_p, pos_p, lw, tb=tb_p, steps_per_batch=sp // tb_p, batch=bp, seq=sp)
        ys, ksn, vsn, us = _layer(True, ys, mods_s, pos_s, lw, tb=bs, steps_per_batch=1, batch=bs, seq=ss,
                                  win_k=cache_win_k[l], win_v=cache_win_v[l],
                                  conv_state=state_conv[l].transpose(1, 0, 2))
        kp_l.append(kp.reshape(bp, sp, N_KV_HEADS, HEAD_DIM)[:, -WINDOW:])
        vp_l.append(vp.reshape(bp, sp, N_KV_HEADS, HEAD_DIM)[:, -WINDOW:])
        cp_l.append(up.reshape(bp, sp // tb_p, SUBLANES, D_CONV)[:, -1, -(CONV_WIDTH - 1):])
        k_new = ksn.reshape(ss, bs, N_KV_HEADS, HEAD_DIM).transpose(1, 0, 2, 3)
        v_new = vsn.reshape(ss, bs, N_KV_HEADS, HEAD_DIM).transpose(1, 0, 2, 3)
        ks_l.append(jnp.concatenate([cache_win_k[l], k_new], axis=1)[:, -WINDOW:])
        vs_l.append(jnp.concatenate([cache_win_v[l], v_new], axis=1)[:, -WINDOW:])
        ext = jnp.concatenate([state_conv[l], us.reshape(ss, bs, D_CONV).transpose(1, 0, 2)], axis=1)
        cs_l.append(ext[:, -(CONV_WIDTH - 1):])
    return (yp.reshape(bp, sp, D_MODEL), ys.reshape(ss, bs, D_MODEL).transpose(1, 0, 2),
            jnp.stack(kp_l), jnp.stack(vp_l), jnp.stack(cp_l), jnp.stack(ks_l), jnp.stack(vs_l), jnp.stack(cs_l))
```

```python
import functools

import numpy as np
import jax
import jax.numpy as jnp
from jax import lax
from jax.experimental import pallas as pl
from jax.experimental.pallas import tpu as pltpu
from jax.experimental.pallas import tpu_sc as plsc

F32 = jnp.float32
BF16 = jnp.bfloat16
HIGHEST = lax.Precision.HIGHEST

D_MODEL = 1024
N_HEADS = 16
HEAD_DIM = 64
N_KV_HEADS = 4
GROUP = N_HEADS // N_KV_HEADS
WINDOW = 128
ATTN_BLOCK = 128
ROPE_THETA = 10000.0
Q_W = N_HEADS * HEAD_DIM
KV_W = N_KV_HEADS * HEAD_DIM
D_CONV = 1024
CONV_WIDTH = 3
PEER_HEADS = 8
N_KEYS = 128
PEER_TOPK = 16
PEER_HALF = 128
PEER_SEL = PEER_HEADS * PEER_TOPK
PAST_LEN = 16384
EPS = 1e-6
NEG = -1e30
IN_SIZES = (Q_W, KV_W, KV_W, D_CONV, D_CONV, D_CONV, D_MODEL, D_MODEL)
IN_OFFS = tuple(int(v) for v in np.cumsum((0,) + IN_SIZES))
D_IN = IN_OFFS[-1]

LANES = 128
SUBLANES = 8
VMEM_LIMIT = 56 * 1024 * 1024
PEER_TOK = 8
PEER_DMA_QUEUES = 2
SC_CORES = 2
SC_SUBCORES = 16
SC_LANES = 16
SC_WORKERS = SC_CORES * SC_SUBCORES
SC_PASS = 32
SC_TOKEN_SHARE = 0.47
PEER_PAIRS = tuple((i, j) for i in range(PEER_TOPK) for j in range(PEER_TOPK) if (i + 1) * (j + 1) <= PEER_TOPK)


def _const_spec(shape):
    nd = len(shape)
    return pl.BlockSpec(shape, lambda *_: (0,) * nd, pipeline_mode=pl.Buffered(1))


def _params(n_grid):
    return pltpu.CompilerParams(dimension_semantics=("arbitrary",) * n_grid, vmem_limit_bytes=VMEM_LIMIT)


def _mod_kernel(c_ref, w_ref, b_ref, o_ref):
    c = c_ref[...]
    s = c * jax.nn.sigmoid(c)
    o_ref[...] = jnp.dot(s, w_ref[...], precision=HIGHEST, preferred_element_type=F32) + b_ref[...]


def _mod(c, w_ada, b_ada):
    rows, n = c.shape[0], w_ada.shape[1]
    tn = 512
    return pl.pallas_call(
        _mod_kernel,
        grid=(n // tn,),
        in_specs=[pl.BlockSpec((rows, D_MODEL), lambda j: (0, 0)),
                  pl.BlockSpec((D_MODEL, tn), lambda j: (0, j)),
                  pl.BlockSpec((1, tn), lambda j: (0, j))],
        out_specs=pl.BlockSpec((rows, tn), lambda j: (0, j)),
        out_shape=jax.ShapeDtypeStruct((rows, n), F32),
        compiler_params=_params(1),
        name="mod",
    )(c, w_ada, b_ada.reshape(1, n))


def _head_norm(z, g_ref, gt_ref, w):
    ss = jnp.dot(z * z, g_ref[...], precision=HIGHEST, preferred_element_type=F32)
    ssb = jnp.dot(ss, gt_ref[...], precision=HIGHEST, preferred_element_type=F32)
    return z * lax.rsqrt(ssb * (1.0 / HEAD_DIM) + EPS) * w


def _rope(x, cos, sin_signed):
    w = x.shape[1]
    lane = lax.broadcasted_iota(jnp.int32, x.shape, 1)
    fwd = pltpu.roll(x, w - HEAD_DIM // 2, axis=1)
    bwd = pltpu.roll(x, HEAD_DIM // 2, axis=1)
    partner = jnp.where((lane % HEAD_DIM) < HEAD_DIM // 2, fwd, bwd)
    reps = w // LANES
    return x * jnp.tile(cos, (1, reps)) + partner * jnp.tile(sin_signed, (1, reps))


def _in_kernel(sample, tb, steps_per_batch, *refs):
    if sample:
        (x_ref, sh_ref, sc_ref, n1_ref, w_ref, qn_ref, kn_ref, cos_ref, sin_ref, cw_ref,
         gq_ref, gqt_ref, gk_ref, gkt_ref, st_ref,
         q_ref, k_ref, v_ref, bc_ref, ga_ref, gc_ref, u_ref, prev_ref) = refs
    else:
        (x_ref, sh_ref, sc_ref, n1_ref, w_ref, qn_ref, kn_ref, cos_ref, sin_ref, cw_ref,
         gq_ref, gqt_ref, gk_ref, gkt_ref,
         q_ref, k_ref, v_ref, bc_ref, ga_ref, gc_ref, u_ref, ext_ref) = refs
    i = pl.program_id(0)
    x = x_ref[...]
    ms = jnp.mean(x * x, axis=-1, keepdims=True)
    h = (x * lax.rsqrt(ms + EPS) * n1_ref[...]) * (1.0 + sc_ref[...]) + sh_ref[...]
    hb = h.astype(BF16)

    def seg(n):
        return jnp.dot(hb, w_ref[:, IN_OFFS[n]:IN_OFFS[n + 1]], preferred_element_type=F32)

    cos, sin = cos_ref[...], sin_ref[...]
    q = _rope(_head_norm(seg(0), gq_ref, gqt_ref, qn_ref[...]), cos, sin)
    q_ref[...] = (q * (HEAD_DIM ** -0.5)).astype(BF16)
    k_ref[...] = _rope(_head_norm(seg(1), gk_ref, gkt_ref, kn_ref[...]), cos, sin)
    v_ref[...] = seg(2)
    u = seg(4) * seg(5)
    w0, w1, w2 = cw_ref[0:1, :], cw_ref[1:2, :], cw_ref[2:3, :]
    if sample:
        @pl.when(i == 0)
        def _():
            prev_ref[...] = st_ref[...]
        p2, p1 = prev_ref[0], prev_ref[1]
        conv = p2 * w0 + p1 * w1 + u * w2
        prev_ref[0] = p1
        prev_ref[1] = u
        u_ref[...] = u
    else:
        @pl.when(i % steps_per_batch == 0)
        def _():
            ext_ref[0:SUBLANES, :] = jnp.zeros((SUBLANES, D_CONV), F32)

        @pl.when(i % steps_per_batch != 0)
        def _():
            ext_ref[0:SUBLANES, :] = ext_ref[tb:tb + SUBLANES, :]
        ext_ref[SUBLANES:tb + SUBLANES, :] = u
        conv = ext_ref[SUBLANES - 2:tb + SUBLANES - 2, :] * w0 + ext_ref[SUBLANES - 1:tb + SUBLANES - 1, :] * w1 + u * w2
        u_ref[...] = u[tb - SUBLANES:tb, :]
    bc_ref[...] = (seg(3) * conv).astype(BF16)
    ga_ref[...] = jax.nn.sigmoid(seg(6))
    gc_ref[...] = jax.nn.sigmoid(seg(7))


def _group_mats(width):
    lane = np.arange(width)
    g = (lane[:, None] // HEAD_DIM == np.arange(LANES)[None, :]).astype(np.float32)
    return jnp.asarray(g), jnp.asarray(g.T)


def _in_proj(sample, x, sh, sc, norm1, w_in_bf, qn, kn, cos, sin, conv_w, state=None, *, tb, steps_per_batch):
    t = x.shape[0]
    steps = t // tb
    gq, gqt = _group_mats(Q_W)
    gk, gkt = _group_mats(KV_W)
    row = lambda w: pl.BlockSpec((tb, w), lambda i: (i, 0))
    if sample:
        mod_spec = pl.BlockSpec((tb, D_MODEL), lambda i: (0, 0))
        rope_spec = pl.BlockSpec((None, 1, LANES), lambda i: (i, 0, 0))
        u_spec, u_rows = row(D_CONV), t
        scratch = [pltpu.VMEM((2, tb, D_CONV), F32)]
    else:
        mod_spec = pl.BlockSpec((None, 1, D_MODEL), lambda i: (i // steps_per_batch, 0, 0))
        rope_spec = pl.BlockSpec((tb, LANES), lambda i: (i % steps_per_batch, 0))
        u_spec, u_rows = pl.BlockSpec((SUBLANES, D_CONV), lambda i: (i, 0)), steps * SUBLANES
        scratch = [pltpu.VMEM((tb + SUBLANES, D_CONV), F32)]
    in_specs = [row(D_MODEL), mod_spec, mod_spec, _const_spec((1, D_MODEL)), _const_spec((D_MODEL, D_IN)),
                _const_spec((1, Q_W)), _const_spec((1, KV_W)), rope_spec, rope_spec, _const_spec((CONV_WIDTH, D_CONV)),
                _const_spec(gq.shape), _const_spec(gqt.shape), _const_spec(gk.shape), _const_spec(gkt.shape)]
    args = [x, sh, sc, norm1, w_in_bf, qn, kn, cos, sin, conv_w, gq, gqt, gk, gkt]
    if sample:
        in_specs.append(_const_spec(state.shape))
        args.append(state)
    return pl.pallas_call(
        functools.partial(_in_kernel, sample, tb, steps_per_batch),
        grid=(steps,),
        in_specs=in_specs,
        out_specs=[row(Q_W), row(KV_W), row(KV_W), row(D_CONV), row(D_MODEL), row(D_MODEL), u_spec],
        out_shape=[jax.ShapeDtypeStruct((t, Q_W), BF16), jax.ShapeDtypeStruct((t, KV_W), F32),
                   jax.ShapeDtypeStruct((t, KV_W), F32), jax.ShapeDtypeStruct((t, D_CONV), BF16),
                   jax.ShapeDtypeStruct((t, D_MODEL), F32), jax.ShapeDtypeStruct((t, D_MODEL), F32),
                   jax.ShapeDtypeStruct((u_rows, D_CONV), F32)],
        scratch_shapes=scratch,
        compiler_params=_params(1),
        name="in_proj_sample" if sample else "in_proj_prompt",
    )(*args)


def _sink_softmax_pv(q, kg, vg, valid, sink):
    s = lax.dot_general(q, kg, (((1,), (1,)), ((), ())), preferred_element_type=F32)
    s = jnp.where(valid, s, NEG)
    m = jnp.maximum(jnp.max(s, axis=1, keepdims=True), sink)
    p = jnp.exp(s - m)
    denom = jnp.sum(p, axis=1, keepdims=True) + jnp.exp(sink - m)
    return jnp.dot(p.astype(BF16), vg, preferred_element_type=F32) / denom


def _attn_prompt_kernel(sink_ref, q_ref, kc_ref, kp_ref, vc_ref, vp_ref, o_ref):
    j = pl.program_id(1)
    q = q_ref[...]
    kb = jnp.concatenate([kp_ref[...], kc_ref[...]], axis=0).astype(BF16)
    vb = jnp.concatenate([vp_ref[...], vc_ref[...]], axis=0).astype(BF16)
    qi = lax.broadcasted_iota(jnp.int32, (ATTN_BLOCK, 2 * ATTN_BLOCK), 0)
    kj = lax.broadcasted_iota(jnp.int32, (ATTN_BLOCK, 2 * ATTN_BLOCK), 1)
    valid = (kj >= qi) & (kj <= qi + WINDOW) & ((kj >= ATTN_BLOCK) | (j > 0))
    outs = []
    for h in range(N_HEADS):
        g = h // GROUP
        kg = kb[:, g * HEAD_DIM:(g + 1) * HEAD_DIM]
        vg = vb[:, g * HEAD_DIM:(g + 1) * HEAD_DIM]
        outs.append(_sink_softmax_pv(q[:, h * HEAD_DIM:(h + 1) * HEAD_DIM], kg, vg, valid, sink_ref[h]))
    o_ref[...] = jnp.concatenate(outs, axis=1).astype(BF16)


def _attn_prompt(q, k, v, sinks, batch, seq):
    nb = seq // ATTN_BLOCK
    cur = lambda w: pl.BlockSpec((ATTN_BLOCK, w), lambda b, j: (b * nb + j, 0))
    prev = lambda w: pl.BlockSpec((ATTN_BLOCK, w), lambda b, j: (b * nb + jnp.maximum(j - 1, 0), 0))
    return pl.pallas_call(
        _attn_prompt_kernel,
        grid=(batch, nb),
        in_specs=[pl.BlockSpec(memory_space=pltpu.SMEM), cur(Q_W), cur(KV_W), prev(KV_W), cur(KV_W), prev(KV_W)],
        out_specs=cur(Q_W),
        out_shape=jax.ShapeDtypeStruct((batch * seq, Q_W), BF16),
        compiler_params=_params(2),
        name="attn_prompt",
    )(sinks, q, k, k, v, v)


def _attn_sample_kernel(bb, n_new, sink_ref, q_ref, kn_ref, vn_ref, kc_ref, vc_ref, o_ref):
    rows = GROUP * n_new
    s_len = 2 * WINDOW
    ri = lax.broadcasted_iota(jnp.int32, (rows, s_len), 0)
    kj = lax.broadcasted_iota(jnp.int32, (rows, s_len), 1)
    t = ri % n_new
    valid = ((kj < WINDOW) & (kj >= t)) | ((kj >= WINDOW) & (kj - WINDOW <= t))
    pad = jnp.zeros((WINDOW - SUBLANES, KV_W), F32)
    for b in range(bb):
        kb = jnp.concatenate([kc_ref[b], kn_ref[b], pad], axis=0).astype(BF16)
        vb = jnp.concatenate([vc_ref[b], vn_ref[b], pad], axis=0).astype(BF16)
        for g in range(N_KV_HEADS):
            kg = kb[:, g * HEAD_DIM:(g + 1) * HEAD_DIM]
            vg = vb[:, g * HEAD_DIM:(g + 1) * HEAD_DIM]
            o_ref[b, g] = _sink_softmax_pv(q_ref[b, g], kg, vg, valid, sink_ref[:, g:g + 1]).astype(BF16)


def _attn_sample(q4, k_new, v_new, cache_k, cache_v, sink_mat, n_new):
    nbatch = q4.shape[0]
    bb = 8 if nbatch % 8 == 0 else nbatch
    rows = GROUP * n_new
    b3 = lambda r, w: pl.BlockSpec((bb, r, w), lambda i: (i, 0, 0))
    qspec = pl.BlockSpec((bb, N_KV_HEADS, rows, HEAD_DIM), lambda i: (i, 0, 0, 0))
    return pl.pallas_call(
        functools.partial(_attn_sample_kernel, bb, n_new),
        grid=(nbatch // bb,),
        in_specs=[_const_spec(sink_mat.shape), qspec, b3(SUBLANES, KV_W), b3(SUBLANES, KV_W),
                  b3(WINDOW, KV_W), b3(WINDOW, KV_W)],
        out_specs=qspec,
        out_shape=jax.ShapeDtypeStruct(q4.shape, BF16),
        compiler_params=_params(1),
        name="attn_sample",
    )(sink_mat, q4, k_new, v_new, cache_k, cache_v)


def _topk_rows(s, k, payload=None):
    r = s.shape[0]
    rows = lax.broadcasted_iota(jnp.int32, s.shape, 0)
    vals, sel = [], []
    for _ in range(k):
        m = jnp.max(s, axis=0, keepdims=True)
        ix = jnp.min(jnp.where(s == m, rows, r), axis=0, keepdims=True)
        hit = rows == ix
        vals.append(m)
        sel.append(ix if payload is None else jnp.max(jnp.where(hit, payload, -1), axis=0, keepdims=True))
        s = jnp.where(hit, -jnp.inf, s)
    return jnp.concatenate(vals, axis=0), jnp.concatenate(sel, axis=0)


def _peer_route(q0, q1, k0, k1):
    nt = (((1,), (1,)), ((), ()))
    s0 = lax.dot_general(k0, q0, nt, preferred_element_type=F32)
    s1 = lax.dot_general(k1, q1, nt, preferred_element_type=F32)
    sv0, si0 = _topk_rows(s0, PEER_TOPK)
    sv1, si1 = _topk_rows(s1, PEER_TOPK)
    tb = s0.shape[1]
    row8 = lax.broadcasted_iota(jnp.int32, (SUBLANES, tb), 0)
    n_grp = -(-len(PEER_PAIRS) // SUBLANES)
    cvs, ces = [], []
    for gi in range(n_grp):
        cv = jnp.full((SUBLANES, tb), -jnp.inf, F32)
        ce = jnp.zeros((SUBLANES, tb), jnp.int32)
        for r in range(SUBLANES):
            c = gi * SUBLANES + r
            if c < len(PEER_PAIRS):
                i, j = PEER_PAIRS[c]
                cv = jnp.where(row8 == r, sv0[i:i + 1, :] + sv1[j:j + 1, :], cv)
                ce = jnp.where(row8 == r, si0[i:i + 1, :] * N_KEYS + si1[j:j + 1, :], ce)
        cvs.append(cv)
        ces.append(ce)
    cs, e = _topk_rows(jnp.concatenate(cvs, axis=0), PEER_TOPK, payload=jnp.concatenate(ces, axis=0))
    p = jnp.exp(cs - cs[0:1, :])
    return e, p / jnp.sum(p, axis=0, keepdims=True)


def _post_kernel(tb, x_ref, at_ref, bc_ref, ga_ref, gc_ref, gt1_ref, sh2_ref, sc2_ref, n2_ref,
                 wa_ref, wc_ref, wo_ref, wpq_ref, sk_ref, x1_ref, h2_ref, e_ref, g_ref, qp_ref):
    merged = (ga_ref[...] * jnp.dot(at_ref[...], wa_ref[...], preferred_element_type=F32)
              + gc_ref[...] * jnp.dot(bc_ref[...], wc_ref[...], preferred_element_type=F32))
    x1 = x_ref[...] + gt1_ref[...] * jnp.dot(merged.astype(BF16), wo_ref[...], preferred_element_type=F32)
    x1_ref[...] = x1
    ms = jnp.mean(x1 * x1, axis=-1, keepdims=True)
    h2 = (x1 * lax.rsqrt(ms + EPS) * n2_ref[...]) * (1.0 + sc2_ref[...]) + sh2_ref[...]
    h2_ref[...] = h2
    qp = jnp.dot(h2.astype(BF16), wpq_ref[...], preferred_element_type=F32).astype(BF16)
    for hp in range(2 * PEER_HEADS):
        qp_ref[hp] = qp[:, hp * PEER_HALF:(hp + 1) * PEER_HALF]

    def head(h, carry):
        e, g = _peer_route(qp_ref[2 * h], qp_ref[2 * h + 1], sk_ref[2 * h], sk_ref[2 * h + 1])
        e_ref[h] = e
        g_ref[h] = g
        return carry

    lax.fori_loop(0, PEER_HEADS, head, 0)


def _post(sample, x, attn, bc, ga, gc, gt1, sh2, sc2, norm2, wa, wc, wo, wpq, sk, *, tb, steps_per_batch):
    t = x.shape[0]
    row = lambda w: pl.BlockSpec((tb, w), lambda i: (i, 0))
    if sample:
        mod_spec = pl.BlockSpec((tb, D_MODEL), lambda i: (0, 0))
    else:
        mod_spec = pl.BlockSpec((None, 1, D_MODEL), lambda i: (i // steps_per_batch, 0, 0))
    sel_spec = pl.BlockSpec((PEER_HEADS, PEER_TOPK, tb), lambda i: (0, 0, i))
    return pl.pallas_call(
        functools.partial(_post_kernel, tb),
        grid=(t // tb,),
        in_specs=[row(D_MODEL), row(Q_W), row(D_CONV), row(D_MODEL), row(D_MODEL), mod_spec, mod_spec, mod_spec,
                  _const_spec((1, D_MODEL)), _const_spec(wa.shape), _const_spec(wc.shape), _const_spec(wo.shape),
                  _const_spec(wpq.shape), _const_spec(sk.shape)],
        out_specs=[row(D_MODEL), row(D_MODEL), sel_spec, sel_spec],
        out_shape=[jax.ShapeDtypeStruct((t, D_MODEL), F32), jax.ShapeDtypeStruct((t, D_MODEL), F32),
                   jax.ShapeDtypeStruct((PEER_HEADS, PEER_TOPK, t), jnp.int32),
                   jax.ShapeDtypeStruct((PEER_HEADS, PEER_TOPK, t), F32)],
        scratch_shapes=[pltpu.VMEM((2 * PEER_HEADS, tb, PEER_HALF), BF16)],
        compiler_params=_params(1),
        name="post_sample" if sample else "post_prompt",
    )(x, attn, bc, ga, gc, gt1, sh2, sc2, norm2, wa, wc, wo, wpq, sk)


def _sublane_sums(p):
    sub = lax.broadcasted_iota(jnp.int32, (SUBLANES, LANES), 0)

    def merge(a, b, sh):
        lo = (sub & sh) == 0
        fa = a + pltpu.roll(a, SUBLANES - sh, axis=0)
        fb = b + pltpu.roll(b, sh, axis=0)
        return jnp.where(lo, fa, fb)

    n = [merge(p[i], p[i + 4], 4) for i in range(4)]
    q = [merge(n[0], n[2], 2), merge(n[1], n[3], 2)]
    return merge(q[0], q[1], 1)


def _u_half(word):
    return lax.bitcast_convert_type(word & jnp.int32(-65536), F32)


def _v_half(word):
    return lax.bitcast_convert_type(word << 16, F32)


def _peer_kernel(n, idx_cur, idx_nxt, h_ref, g_ref, uv_hbm, o_ref, buf0, buf1, wbuf, sem):
    i = pl.program_id(0)
    phase_rows = PEER_TOK * PEER_SEL
    bufs = (buf0, buf1)

    def token_rows(s, t):
        return bufs[s].at[pl.ds(t * PEER_SEL, PEER_SEL)]

    def issue_token(idx_ref, off, s, t):
        for k in range(PEER_SEL):
            r = t * PEER_SEL + k
            pltpu.make_async_copy(uv_hbm.at[idx_ref[off + r]], bufs[s].at[r], sem.at[s, t]).start(
                priority=k % PEER_DMA_QUEUES)

    def wait_token(s, t):
        pltpu.make_async_copy(uv_hbm.at[pl.ds(0, PEER_SEL)], token_rows(s, t), sem.at[s, t]).wait()

    @pl.when(i == 0)
    def _():
        def body(r, c):
            pltpu.make_async_copy(uv_hbm.at[idx_cur[r]], buf0.at[r], sem.at[0, r // PEER_SEL]).start()
            return c
        lax.fori_loop(0, phase_rows, body, 0)

    eye = (lax.broadcasted_iota(jnp.int32, (PEER_SEL, PEER_SEL), 0)
           == lax.broadcasted_iota(jnp.int32, (PEER_SEL, PEER_SEL), 1)).astype(F32)
    gcol = lax.dot_general(eye, g_ref[...], (((1,), (1,)), ((), ())), precision=HIGHEST, preferred_element_type=F32)
    for s in range(2):
        for t in range(PEER_TOK):
            wait_token(s, t)
            if s == 0:
                issue_token(idx_cur, phase_rows, 1, t)
            else:
                issue_token(idx_nxt, 0, 0, t)
            tok = s * PEER_TOK + t
            h = h_ref[tok]
            base = t * PEER_SEL
            parts = []
            for k0 in range(0, PEER_SEL, SUBLANES):
                parts.append(_sublane_sums([_u_half(bufs[s][base + k0 + j]) * h for j in range(SUBLANES)]))
            a = jnp.sum(jnp.concatenate(parts, axis=0), axis=1, keepdims=True)
            w = gcol[:, tok:tok + 1] * jax.nn.gelu(a)
            wbuf[...] = jnp.broadcast_to(w, (PEER_SEL, LANES))
            accs = [jnp.zeros((SUBLANES, LANES), F32) for _ in range(4)]
            for k in range(PEER_SEL):
                accs[k % 4] = accs[k % 4] + wbuf[k:k + 1, :] * _v_half(bufs[s][base + k])
            o_ref[tok] = (accs[0] + accs[1]) + (accs[2] + accs[3])

    @pl.when(i == n - 1)
    def _():
        for t in range(PEER_TOK):
            wait_token(0, t)


def _peer(idx_flat, h3, g_tok, uv3):
    t = h3.shape[0]
    step_tok = 2 * PEER_TOK
    n = t // step_tok
    rows = step_tok * PEER_SEL
    tile = pl.BlockSpec((step_tok, SUBLANES, LANES), lambda i: (i, 0, 0))
    return pl.pallas_call(
        functools.partial(_peer_kernel, n),
        grid=(n,),
        in_specs=[pl.BlockSpec((rows,), lambda i: (i,), memory_space=pltpu.SMEM),
                  pl.BlockSpec((rows,), lambda i: (jnp.minimum(i + 1, n - 1),), memory_space=pltpu.SMEM),
                  tile,
                  pl.BlockSpec((step_tok, PEER_SEL), lambda i: (i, 0)),
                  pl.BlockSpec(memory_space=pl.ANY)],
        out_specs=tile,
        out_shape=jax.ShapeDtypeStruct((t, SUBLANES, LANES), F32),
        scratch_shapes=[pltpu.VMEM((PEER_TOK * PEER_SEL, SUBLANES, LANES), jnp.int32),
                        pltpu.VMEM((PEER_TOK * PEER_SEL, SUBLANES, LANES), jnp.int32),
                        pltpu.VMEM((PEER_SEL, LANES), F32),
                        pltpu.SemaphoreType.DMA((2, PEER_TOK))],
        compiler_params=pltpu.CompilerParams(dimension_semantics=("arbitrary",), vmem_limit_bytes=VMEM_LIMIT,
                                             disable_bounds_checks=True),
        name="peer_mix",
    )(idx_flat, idx_flat, h3, g_tok, uv3)


def _gelu_tanh_via_exp(a):
    z = 0.7978845608028654 * (a + 0.044715 * (a * a * a))
    t = 1.0 - 2.0 / (jnp.exp(2.0 * z) + 1.0)
    return 0.5 * a * (1.0 + t)


def _sc_peer(tab, idx, h, g):
    t_all = idx.shape[0]
    per_w = t_all // SC_WORKERS
    n_pass = PEER_SEL // SC_PASS
    n_chunk = D_MODEL // SC_LANES
    quarter = 16 * SC_LANES
    mesh = plsc.VectorSubcoreMesh(core_axis_name="c", subcore_axis_name="s")

    @functools.partial(
        pl.kernel, mesh=mesh,
        out_type=jax.ShapeDtypeStruct((t_all, D_MODEL), F32),
        scratch_types=[pltpu.VMEM((2, PEER_SEL), jnp.int32),
                       pltpu.VMEM((2, SC_PASS, D_MODEL), jnp.int32),
                       pltpu.VMEM((2, D_MODEL), F32),
                       pltpu.VMEM((2, PEER_SEL), F32),
                       pltpu.VMEM((SC_PASS, SC_LANES), F32),
                       pltpu.VMEM((PEER_SEL,), F32),
                       pltpu.VMEM((D_MODEL,), F32),
                       pltpu.SemaphoreType.DMA((2,)),
                       pltpu.SemaphoreType.DMA((2,))],
        compiler_params=pltpu.CompilerParams(needs_layout_passes=False),
    )
    def k(tab_hbm, idx_hbm, h_hbm, g_hbm, out_hbm, idx_v, rows_v, h_v, g_v, ap_v, w_v, o_v, tsem, rsem):
        wid = lax.axis_index("s") * SC_CORES + lax.axis_index("c")
        lane = lax.iota(jnp.int32, SC_LANES)
        zero = jnp.zeros((SC_LANES,), F32)
        tok0 = wid * per_w

        def token_inputs(tok, slot):
            return (pltpu.make_async_copy(idx_hbm.at[tok], idx_v.at[slot], tsem.at[slot]),
                    pltpu.make_async_copy(h_hbm.at[tok], h_v.at[slot], tsem.at[slot]),
                    pltpu.make_async_copy(g_hbm.at[tok], g_v.at[slot], tsem.at[slot]))

        def gather(slot, q, b):
            ids = idx_v.at[slot].at[pl.ds(q * SC_PASS, SC_PASS)]
            return pltpu.make_async_copy(tab_hbm.at[ids], rows_v.at[b], rsem.at[b])

        for c in token_inputs(tok0, 0):
            c.start()
        for c in token_inputs(tok0, 0):
            c.wait()
        gather(0, 0, 0).start()

        def token(ti, carry):
            ts = lax.rem(ti, 2)
            tn = 1 - ts
            tok = tok0 + ti
            nxt = tok0 + jnp.minimum(ti + 1, per_w - 1)
            for c in token_inputs(nxt, tn):
                c.start()
            for j in range(n_chunk):
                o_v[pl.ds(j * SC_LANES, SC_LANES)] = zero
            for q in range(n_pass):
                b = q % 2
                gather(ts, q, b).wait()
                if q + 1 < n_pass:
                    gather(ts, q + 1, 1 - b).start()
                else:
                    for c in token_inputs(nxt, tn):
                        c.wait()
                    gather(tn, 0, 1 - b).start()
                for jc in range(D_MODEL // quarter):
                    hregs = [h_v[ts, pl.ds(jc * quarter + j * SC_LANES, SC_LANES)] for j in range(16)]

                    def dot_rows(kk, jc=jc, hregs=hregs, b=b):
                        part = [None] * 4
                        for j in range(16):
                            p = _u_half(rows_v[b, kk, pl.ds(jc * quarter + j * SC_LANES, SC_LANES)]) * hregs[j]
                            part[j % 4] = p if part[j % 4] is None else part[j % 4] + p
                        acc = (part[0] + part[1]) + (part[2] + part[3])
                        if jc == 0:
                            ap_v[kk, :] = acc
                        else:
                            plsc.addupdate(ap_v.at[kk], acc)

                    plsc.parallel_loop(0, SC_PASS)(dot_rows)
                for m in range(SC_PASS // SC_LANES):
                    rowi = lane + m * SC_LANES
                    a = zero
                    for col in range(SC_LANES):
                        a = a + plsc.load_gather(ap_v, [rowi, jnp.full((SC_LANES,), col, jnp.int32)])
                    off = q * SC_PASS + m * SC_LANES
                    w_v[pl.ds(off, SC_LANES)] = g_v[ts, pl.ds(off, SC_LANES)] * _gelu_tanh_via_exp(a)
                for jc in range(D_MODEL // quarter):
                    def mix_rows(kk, accs, q=q, jc=jc, b=b):
                        ws = plsc.load_gather(w_v, [jnp.full((SC_LANES,), q * SC_PASS, jnp.int32) + kk])
                        return tuple(accs[j] + ws * _v_half(rows_v[b, kk, pl.ds(jc * quarter + j * SC_LANES, SC_LANES)])
                                     for j in range(16))

                    accs = plsc.parallel_loop(0, SC_PASS, carry=tuple(zero for _ in range(16)))(mix_rows)
                    for j in range(16):
                        plsc.addupdate(o_v.at[pl.ds(jc * quarter + j * SC_LANES, SC_LANES)], accs[j])
            pltpu.sync_copy(o_v, out_hbm.at[tok])
            return carry

        lax.fori_loop(0, per_w, token, 0)
        gather(0, 0, 0).wait()

    return k(tab, idx, h, g)


def _final_kernel(x_ref, gt_ref, o_ref, y_ref):
    y_ref[...] = x_ref[...] + gt_ref[...] * o_ref[...]


def _final(sample, x1, gt2, out, *, tb, steps_per_batch):
    t = x1.shape[0]
    row = pl.BlockSpec((tb, D_MODEL), lambda i: (i, 0))
    if sample:
        mod_spec = pl.BlockSpec((tb, D_MODEL), lambda i: (0, 0))
    else:
        mod_spec = pl.BlockSpec((None, 1, D_MODEL), lambda i: (i // steps_per_batch, 0, 0))
    return pl.pallas_call(
        _final_kernel,
        grid=(t // tb,),
        in_specs=[row, mod_spec, row],
        out_specs=row,
        out_shape=jax.ShapeDtypeStruct((t, D_MODEL), F32),
        compiler_params=_params(1),
        name="final_sample" if sample else "final_prompt",
    )(x1, gt2, out)


def _rope_tables(pos):
    half = HEAD_DIM // 2
    inv = jnp.power(jnp.float32(ROPE_THETA), -jnp.arange(half, dtype=F32) / half)
    ang = pos.astype(F32)[:, None] * inv[None, :]
    lane = np.arange(LANES)
    sign = jnp.asarray(np.where((lane % HEAD_DIM) < half, -1.0, 1.0).astype(np.float32))
    cos = jnp.cos(ang)[:, lane % half]
    sin = jnp.sin(ang)[:, lane % half] * sign[None, :]
    return cos, sin


def _layer(sample, x, mods, pos, lw, *, tb, steps_per_batch, batch, seq, win_k=None, win_v=None, conv_state=None):
    sh1, sc1, gt1, sh2, sc2, gt2 = mods
    cos, sin = _rope_tables(pos)
    if sample:
        cos, sin = cos[:, None, :], sin[:, None, :]
    q, k, v, bc, ga, gc, u_out = _in_proj(sample, x, sh1, sc1, lw["norm1"], lw["w_in"], lw["qn"], lw["kn"], cos, sin,
                                          lw["conv_w"], conv_state, tb=tb, steps_per_batch=steps_per_batch)
    if sample:
        n_new, nb = seq, batch
        q4 = q.reshape(n_new, nb, N_KV_HEADS, GROUP, HEAD_DIM).transpose(1, 2, 3, 0, 4).reshape(
            nb, N_KV_HEADS, GROUP * n_new, HEAD_DIM)
        to_b = lambda a: jnp.pad(a.reshape(n_new, nb, KV_W).transpose(1, 0, 2), ((0, 0), (0, SUBLANES - n_new), (0, 0)))
        sink_mat = jnp.repeat(lw["sinks"].reshape(N_KV_HEADS, GROUP), n_new, axis=1).T
        o4 = _attn_sample(q4, to_b(k), to_b(v), win_k.reshape(nb, WINDOW, KV_W), win_v.reshape(nb, WINDOW, KV_W),
                          sink_mat, n_new)
        attn = o4.reshape(nb, N_KV_HEADS, GROUP, n_new, HEAD_DIM).transpose(3, 0, 1, 2, 4).reshape(n_new * nb, Q_W)
    else:
        attn = _attn_prompt(q, k, v, lw["sinks"], batch, seq)
    x1, h2, e, g = _post(sample, x, attn, bc, ga, gc, gt1, sh2, sc2, lw["norm2"], lw["wa"], lw["wc"], lw["wo"],
                         lw["wpq"], lw["sk"], tb=tb, steps_per_batch=steps_per_batch)
    t = x.shape[0]
    idx = e.reshape(PEER_SEL, t).T
    g_tok = g.reshape(PEER_SEL, t).T
    t_sc = 0 if sample else int(t * SC_TOKEN_SHARE) // (2 * PEER_TOK * SC_WORKERS) * (2 * PEER_TOK * SC_WORKERS)
    t_tc = t - t_sc
    out = _peer(idx[:t_tc].reshape(t_tc * PEER_SEL), h2[:t_tc].reshape(t_tc, SUBLANES, LANES), g_tok[:t_tc],
                lw["uv"].reshape(-1, SUBLANES, LANES)).reshape(t_tc, D_MODEL)
    if t_sc:
        out = jnp.concatenate([out, _sc_peer(lw["uv"], idx[t_tc:], h2[t_tc:], g_tok[t_tc:])], axis=0)
    y = _final(sample, x1, gt2, out, tb=tb, steps_per_batch=steps_per_batch)
    return y, k, v, u_out


def _pack_uv(u, v):
    ub = lax.bitcast_convert_type(u.astype(BF16), jnp.uint16).astype(jnp.uint32)
    vb = lax.bitcast_convert_type(v.astype(BF16), jnp.uint16).astype(jnp.uint32)
    return lax.bitcast_convert_type((ub << 16) | vb, jnp.int32)


def kernel(x_prompt, x_sample, c_prompt, c_sample, cache_win_k, cache_win_v, state_conv, w_ada, b_ada, norm1, w_in, q_norm, k_norm, attn_sinks, conv_w, w_attn_o, w_conv_o, w_out, norm2, w_peer_q, peer_sub_keys, peer_u, peer_v):
    depth = w_ada.shape[0]
    bp, sp, _ = x_prompt.shape
    bs, ss, _ = x_sample.shape
    assert sp % ATTN_BLOCK == 0 and ss <= SUBLANES and (bs * ss) % (2 * PEER_TOK) == 0
    tb_p = 256 if sp % 256 == 0 else ATTN_BLOCK
    pos_p = jnp.arange(sp)
    pos_s = PAST_LEN + jnp.arange(ss)
    yp = x_prompt.reshape(bp * sp, D_MODEL)
    ys = x_sample.transpose(1, 0, 2).reshape(ss * bs, D_MODEL)
    kp_l, vp_l, cp_l, ks_l, vs_l, cs_l = [], [], [], [], [], []
    for l in range(depth):
        lw = {
            "norm1": norm1[l].reshape(1, D_MODEL), "norm2": norm2[l].reshape(1, D_MODEL),
            "w_in": w_in[l].astype(BF16),
            "qn": jnp.tile(q_norm[l], N_HEADS).reshape(1, Q_W), "kn": jnp.tile(k_norm[l], N_KV_HEADS).reshape(1, KV_W),
            "sinks": attn_sinks[l], "conv_w": conv_w[l],
            "wa": w_attn_o[l].astype(BF16), "wc": w_conv_o[l].astype(BF16), "wo": w_out[l].astype(BF16),
            "wpq": w_peer_q[l].astype(BF16),
            "sk": peer_sub_keys[l].reshape(2 * PEER_HEADS, N_KEYS, PEER_HALF).astype(BF16),
            "uv": _pack_uv(peer_u[l], peer_v[l]),
        }
        c_all = jnp.concatenate([c_prompt, c_sample], axis=0)
        pad = (-c_all.shape[0]) % SUBLANES
        mod = _mod(jnp.pad(c_all, ((0, pad), (0, 0))), w_ada[l], b_ada[l])
        mods_p = tuple(m.reshape(bp, 1, D_MODEL) for m in jnp.split(mod[:bp], 6, axis=-1))
        mods_s = tuple(jnp.split(mod[bp:bp + bs], 6, axis=-1))
        ys, ksn, vsn, us = _layer(True, ys, mods_s, pos_s, lw, tb=bs, steps_per_batch=1, batch=bs, seq=ss,
                                  win_k=cache_win_k[l], win_v=cache_win_v[l],
                                  conv_state=state_conv[l].transpose(1, 0, 2))
        yp, kp, vp, up = _layer(False, yp, mods_p, pos_p, lw, tb=tb_p, steps_per_batch=sp // tb_p, batch=bp, seq=sp)
        kp_l.append(kp.reshape(bp, sp, N_KV_HEADS, HEAD_DIM)[:, -WINDOW:])
        vp_l.append(vp.reshape(bp, sp, N_KV_HEADS, HEAD_DIM)[:, -WINDOW:])
        cp_l.append(up.reshape(bp, sp // tb_p, SUBLANES, D_CONV)[:, -1, -(CONV_WIDTH - 1):])
        k_new = ksn.reshape(ss, bs, N_KV_HEADS, HEAD_DIM).transpose(1, 0, 2, 3)
        v_new = vsn.reshape(ss, bs, N_KV_HEADS, HEAD_DIM).transpose(1, 0, 2, 3)
        ks_l.append(jnp.concatenate([cache_win_k[l], k_new], axis=1)[:, -WINDOW:])
        vs_l.append(jnp.concatenate([cache_win_v[l], v_new], axis=1)[:, -WINDOW:])
        ext = jnp.concatenate([state_conv[l], us.reshape(ss, bs, D_CONV).transpose(1, 0, 2)], axis=1)
        cs_l.append(ext[:, -(CONV_WIDTH - 1):])
    return (yp.reshape(bp, sp, D_MODEL), ys.reshape(ss, bs, D_MODEL).transpose(1, 0, 2),
            jnp.stack(kp_l), jnp.stack(vp_l), jnp.stack(cp_l), jnp.stack(ks_l), jnp.stack(vs_l), jnp.stack(cs_l))
```

```python
import functools

import numpy as np
import jax
import jax.numpy as jnp
from jax import lax
from jax.experimental import pallas as pl
from jax.experimental.pallas import tpu as pltpu
from jax.experimental.pallas import tpu_sc as plsc

F32 = jnp.float32
BF16 = jnp.bfloat16
HIGHEST = lax.Precision.HIGHEST

D_MODEL = 1024
N_HEADS = 16
HEAD_DIM = 64
N_KV_HEADS = 4
GROUP = N_HEADS // N_KV_HEADS
WINDOW = 128
ATTN_BLOCK = 128
ROPE_THETA = 10000.0
Q_W = N_HEADS * HEAD_DIM
KV_W = N_KV_HEADS * HEAD_DIM
D_CONV = 1024
CONV_WIDTH = 3
PEER_HEADS = 8
N_KEYS = 128
PEER_TOPK = 16
PEER_HALF = 128
PEER_SEL = PEER_HEADS * PEER_TOPK
PAST_LEN = 16384
EPS = 1e-6
NEG = -1e30
IN_SIZES = (Q_W, KV_W, KV_W, D_CONV, D_CONV, D_CONV, D_MODEL, D_MODEL)
IN_OFFS = tuple(int(v) for v in np.cumsum((0,) + IN_SIZES))
D_IN = IN_OFFS[-1]

LANES = 128
SUBLANES = 8
VMEM_LIMIT = 56 * 1024 * 1024
PEER_TOK = 8
PEER_DMA_QUEUES = 2
SC_CORES = 2
SC_SUBCORES = 16
SC_LANES = 16
SC_WORKERS = SC_CORES * SC_SUBCORES
SC_PASS = 32
SC_TOKEN_SHARE = 0.62
PEER_PAIRS = tuple((i, j) for i in range(PEER_TOPK) for j in range(PEER_TOPK) if (i + 1) * (j + 1) <= PEER_TOPK)


def _const_spec(shape):
    nd = len(shape)
    return pl.BlockSpec(shape, lambda *_: (0,) * nd, pipeline_mode=pl.Buffered(1))


def _params(n_grid):
    return pltpu.CompilerParams(dimension_semantics=("arbitrary",) * n_grid, vmem_limit_bytes=VMEM_LIMIT)


def _mod_kernel(c_ref, w_ref, b_ref, o_ref):
    c = c_ref[...]
    s = c * jax.nn.sigmoid(c)
    o_ref[...] = jnp.dot(s, w_ref[...], precision=HIGHEST, preferred_element_type=F32) + b_ref[...]


def _mod(c, w_ada, b_ada):
    rows, n = c.shape[0], w_ada.shape[1]
    tn = 512
    return pl.pallas_call(
        _mod_kernel,
        grid=(n // tn,),
        in_specs=[pl.BlockSpec((rows, D_MODEL), lambda j: (0, 0)),
                  pl.BlockSpec((D_MODEL, tn), lambda j: (0, j)),
                  pl.BlockSpec((1, tn), lambda j: (0, j))],
        out_specs=pl.BlockSpec((rows, tn), lambda j: (0, j)),
        out_shape=jax.ShapeDtypeStruct((rows, n), F32),
        compiler_params=_params(1),
        name="mod",
    )(c, w_ada, b_ada.reshape(1, n))


def _head_norm(z, g_ref, gt_ref, w):
    ss = jnp.dot(z * z, g_ref[...], precision=HIGHEST, preferred_element_type=F32)
    ssb = jnp.dot(ss, gt_ref[...], precision=HIGHEST, preferred_element_type=F32)
    return z * lax.rsqrt(ssb * (1.0 / HEAD_DIM) + EPS) * w


def _rope(x, cos, sin_signed):
    w = x.shape[1]
    lane = lax.broadcasted_iota(jnp.int32, x.shape, 1)
    fwd = pltpu.roll(x, w - HEAD_DIM // 2, axis=1)
    bwd = pltpu.roll(x, HEAD_DIM // 2, axis=1)
    partner = jnp.where((lane % HEAD_DIM) < HEAD_DIM // 2, fwd, bwd)
    reps = w // LANES
    return x * jnp.tile(cos, (1, reps)) + partner * jnp.tile(sin_signed, (1, reps))


def _in_kernel(sample, tb, steps_per_batch, *refs):
    if sample:
        (x_ref, sh_ref, sc_ref, n1_ref, w_ref, qn_ref, kn_ref, cos_ref, sin_ref, cw_ref,
         gq_ref, gqt_ref, gk_ref, gkt_ref, st_ref,
         q_ref, k_ref, v_ref, bc_ref, ga_ref, gc_ref, u_ref, prev_ref) = refs
    else:
        (x_ref, sh_ref, sc_ref, n1_ref, w_ref, qn_ref, kn_ref, cos_ref, sin_ref, cw_ref,
         gq_ref, gqt_ref, gk_ref, gkt_ref,
         q_ref, k_ref, v_ref, bc_ref, ga_ref, gc_ref, u_ref, ext_ref) = refs
    i = pl.program_id(0)
    x = x_ref[...]
    ms = jnp.mean(x * x, axis=-1, keepdims=True)
    h = (x * lax.rsqrt(ms + EPS) * n1_ref[...]) * (1.0 + sc_ref[...]) + sh_ref[...]
    hb = h.astype(BF16)

    def seg(n):
        return jnp.dot(hb, w_ref[:, IN_OFFS[n]:IN_OFFS[n + 1]], preferred_element_type=F32)

    cos, sin = cos_ref[...], sin_ref[...]
    q = _rope(_head_norm(seg(0), gq_ref, gqt_ref, qn_ref[...]), cos, sin)
    q_ref[...] = (q * (HEAD_DIM ** -0.5)).astype(BF16)
    k_ref[...] = _rope(_head_norm(seg(1), gk_ref, gkt_ref, kn_ref[...]), cos, sin)
    v_ref[...] = seg(2)
    u = seg(4) * seg(5)
    w0, w1, w2 = cw_ref[0:1, :], cw_ref[1:2, :], cw_ref[2:3, :]
    if sample:
        @pl.when(i == 0)
        def _():
            prev_ref[...] = st_ref[...]
        p2, p1 = prev_ref[0], prev_ref[1]
        conv = p2 * w0 + p1 * w1 + u * w2
        prev_ref[0] = p1
        prev_ref[1] = u
        u_ref[...] = u
    else:
        @pl.when(i % steps_per_batch == 0)
        def _():
            ext_ref[0:SUBLANES, :] = jnp.zeros((SUBLANES, D_CONV), F32)

        @pl.when(i % steps_per_batch != 0)
        def _():
            ext_ref[0:SUBLANES, :] = ext_ref[tb:tb + SUBLANES, :]
        ext_ref[SUBLANES:tb + SUBLANES, :] = u
        conv = ext_ref[SUBLANES - 2:tb + SUBLANES - 2, :] * w0 + ext_ref[SUBLANES - 1:tb + SUBLANES - 1, :] * w1 + u * w2
        u_ref[...] = u[tb - SUBLANES:tb, :]
    bc_ref[...] = (seg(3) * conv).astype(BF16)
    ga_ref[...] = jax.nn.sigmoid(seg(6))
    gc_ref[...] = jax.nn.sigmoid(seg(7))


def _group_mats(width):
    lane = np.arange(width)
    g = (lane[:, None] // HEAD_DIM == np.arange(LANES)[None, :]).astype(np.float32)
    return jnp.asarray(g), jnp.asarray(g.T)


def _in_proj(sample, x, sh, sc, norm1, w_in_bf, qn, kn, cos, sin, conv_w, state=None, *, tb, steps_per_batch):
    t = x.shape[0]
    steps = t // tb
    gq, gqt = _group_mats(Q_W)
    gk, gkt = _group_mats(KV_W)
    row = lambda w: pl.BlockSpec((tb, w), lambda i: (i, 0))
    if sample:
        mod_spec = pl.BlockSpec((tb, D_MODEL), lambda i: (0, 0))
        rope_spec = pl.BlockSpec((None, 1, LANES), lambda i: (i, 0, 0))
        u_spec, u_rows = row(D_CONV), t
        scratch = [pltpu.VMEM((2, tb, D_CONV), F32)]
    else:
        mod_spec = pl.BlockSpec((None, 1, D_MODEL), lambda i: (i // steps_per_batch, 0, 0))
        rope_spec = pl.BlockSpec((tb, LANES), lambda i: (i % steps_per_batch, 0))
        u_spec, u_rows = pl.BlockSpec((SUBLANES, D_CONV), lambda i: (i, 0)), steps * SUBLANES
        scratch = [pltpu.VMEM((tb + SUBLANES, D_CONV), F32)]
    in_specs = [row(D_MODEL), mod_spec, mod_spec, _const_spec((1, D_MODEL)), _const_spec((D_MODEL, D_IN)),
                _const_spec((1, Q_W)), _const_spec((1, KV_W)), rope_spec, rope_spec, _const_spec((CONV_WIDTH, D_CONV)),
                _const_spec(gq.shape), _const_spec(gqt.shape), _const_spec(gk.shape), _const_spec(gkt.shape)]
    args = [x, sh, sc, norm1, w_in_bf, qn, kn, cos, sin, conv_w, gq, gqt, gk, gkt]
    if sample:
        in_specs.append(_const_spec(state.shape))
        args.append(state)
    return pl.pallas_call(
        functools.partial(_in_kernel, sample, tb, steps_per_batch),
        grid=(steps,),
        in_specs=in_specs,
        out_specs=[row(Q_W), row(KV_W), row(KV_W), row(D_CONV), row(D_MODEL), row(D_MODEL), u_spec],
        out_shape=[jax.ShapeDtypeStruct((t, Q_W), BF16), jax.ShapeDtypeStruct((t, KV_W), F32),
                   jax.ShapeDtypeStruct((t, KV_W), F32), jax.ShapeDtypeStruct((t, D_CONV), BF16),
                   jax.ShapeDtypeStruct((t, D_MODEL), F32), jax.ShapeDtypeStruct((t, D_MODEL), F32),
                   jax.ShapeDtypeStruct((u_rows, D_CONV), F32)],
        scratch_shapes=scratch,
        compiler_params=_params(1),
        name="in_proj_sample" if sample else "in_proj_prompt",
    )(*args)


def _sink_softmax_pv(q, kg, vg, valid, sink):
    s = lax.dot_general(q, kg, (((1,), (1,)), ((), ())), preferred_element_type=F32)
    s = jnp.where(valid, s, NEG)
    m = jnp.maximum(jnp.max(s, axis=1, keepdims=True), sink)
    p = jnp.exp(s - m)
    denom = jnp.sum(p, axis=1, keepdims=True) + jnp.exp(sink - m)
    return jnp.dot(p.astype(BF16), vg, preferred_element_type=F32) / denom


def _attn_prompt_kernel(sink_ref, q_ref, kc_ref, kp_ref, vc_ref, vp_ref, o_ref):
    j = pl.program_id(1)
    q = q_ref[...]
    kb = jnp.concatenate([kp_ref[...], kc_ref[...]], axis=0).astype(BF16)
    vb = jnp.concatenate([vp_ref[...], vc_ref[...]], axis=0).astype(BF16)
    qi = lax.broadcasted_iota(jnp.int32, (ATTN_BLOCK, 2 * ATTN_BLOCK), 0)
    kj = lax.broadcasted_iota(jnp.int32, (ATTN_BLOCK, 2 * ATTN_BLOCK), 1)
    valid = (kj >= qi) & (kj <= qi + WINDOW) & ((kj >= ATTN_BLOCK) | (j > 0))
    outs = []
    for h in range(N_HEADS):
        g = h // GROUP
        kg = kb[:, g * HEAD_DIM:(g + 1) * HEAD_DIM]
        vg = vb[:, g * HEAD_DIM:(g + 1) * HEAD_DIM]
        outs.append(_sink_softmax_pv(q[:, h * HEAD_DIM:(h + 1) * HEAD_DIM], kg, vg, valid, sink_ref[h]))
    o_ref[...] = jnp.concatenate(outs, axis=1).astype(BF16)


def _attn_prompt(q, k, v, sinks, batch, seq):
    nb = seq // ATTN_BLOCK
    cur = lambda w: pl.BlockSpec((ATTN_BLOCK, w), lambda b, j: (b * nb + j, 0))
    prev = lambda w: pl.BlockSpec((ATTN_BLOCK, w), lambda b, j: (b * nb + jnp.maximum(j - 1, 0), 0))
    return pl.pallas_call(
        _attn_prompt_kernel,
        grid=(batch, nb),
        in_specs=[pl.BlockSpec(memory_space=pltpu.SMEM), cur(Q_W), cur(KV_W), prev(KV_W), cur(KV_W), prev(KV_W)],
        out_specs=cur(Q_W),
        out_shape=jax.ShapeDtypeStruct((batch * seq, Q_W), BF16),
        compiler_params=_params(2),
        name="attn_prompt",
    )(sinks, q, k, k, v, v)


def _attn_sample_kernel(bb, n_new, sink_ref, q_ref, kn_ref, vn_ref, kc_ref, vc_ref, o_ref):
    rows = GROUP * n_new
    s_len = 2 * WINDOW
    ri = lax.broadcasted_iota(jnp.int32, (rows, s_len), 0)
    kj = lax.broadcasted_iota(jnp.int32, (rows, s_len), 1)
    t = ri % n_new
    valid = ((kj < WINDOW) & (kj >= t)) | ((kj >= WINDOW) & (kj - WINDOW <= t))
    pad = jnp.zeros((WINDOW - SUBLANES, KV_W), F32)
    for b in range(bb):
        kb = jnp.concatenate([kc_ref[b], kn_ref[b], pad], axis=0).astype(BF16)
        vb = jnp.concatenate([vc_ref[b], vn_ref[b], pad], axis=0).astype(BF16)
        for g in range(N_KV_HEADS):
            kg = kb[:, g * HEAD_DIM:(g + 1) * HEAD_DIM]
            vg = vb[:, g * HEAD_DIM:(g + 1) * HEAD_DIM]
            o_ref[b, g] = _sink_softmax_pv(q_ref[b, g], kg, vg, valid, sink_ref[:, g:g + 1]).astype(BF16)


def _attn_sample(q4, k_new, v_new, cache_k, cache_v, sink_mat, n_new):
    nbatch = q4.shape[0]
    bb = 8 if nbatch % 8 == 0 else nbatch
    rows = GROUP * n_new
    b3 = lambda r, w: pl.BlockSpec((bb, r, w), lambda i: (i, 0, 0))
    qspec = pl.BlockSpec((bb, N_KV_HEADS, rows, HEAD_DIM), lambda i: (i, 0, 0, 0))
    return pl.pallas_call(
        functools.partial(_attn_sample_kernel, bb, n_new),
        grid=(nbatch // bb,),
        in_specs=[_const_spec(sink_mat.shape), qspec, b3(SUBLANES, KV_W), b3(SUBLANES, KV_W),
                  b3(WINDOW, KV_W), b3(WINDOW, KV_W)],
        out_specs=qspec,
        out_shape=jax.ShapeDtypeStruct(q4.shape, BF16),
        compiler_params=_params(1),
        name="attn_sample",
    )(sink_mat, q4, k_new, v_new, cache_k, cache_v)


def _topk_rows(s, k, payload=None):
    r = s.shape[0]
    rows = lax.broadcasted_iota(jnp.int32, s.shape, 0)
    vals, sel = [], []
    for _ in range(k):
        m = jnp.max(s, axis=0, keepdims=True)
        ix = jnp.min(jnp.where(s == m, rows, r), axis=0, keepdims=True)
        hit = rows == ix
        vals.append(m)
        sel.append(ix if payload is None else jnp.max(jnp.where(hit, payload, -1), axis=0, keepdims=True))
        s = jnp.where(hit, -jnp.inf, s)
    return jnp.concatenate(vals, axis=0), jnp.concatenate(sel, axis=0)


def _peer_route(q0, q1, k0, k1):
    nt = (((1,), (1,)), ((), ()))
    s0 = lax.dot_general(k0, q0, nt, preferred_element_type=F32)
    s1 = lax.dot_general(k1, q1, nt, preferred_element_type=F32)
    sv0, si0 = _topk_rows(s0, PEER_TOPK)
    sv1, si1 = _topk_rows(s1, PEER_TOPK)
    tb = s0.shape[1]
    row8 = lax.broadcasted_iota(jnp.int32, (SUBLANES, tb), 0)
    n_grp = -(-len(PEER_PAIRS) // SUBLANES)
    cvs, ces = [], []
    for gi in range(n_grp):
        cv = jnp.full((SUBLANES, tb), -jnp.inf, F32)
        ce = jnp.zeros((SUBLANES, tb), jnp.int32)
        for r in range(SUBLANES):
            c = gi * SUBLANES + r
            if c < len(PEER_PAIRS):
                i, j = PEER_PAIRS[c]
                cv = jnp.where(row8 == r, sv0[i:i + 1, :] + sv1[j:j + 1, :], cv)
                ce = jnp.where(row8 == r, si0[i:i + 1, :] * N_KEYS + si1[j:j + 1, :], ce)
        cvs.append(cv)
        ces.append(ce)
    cs, e = _topk_rows(jnp.concatenate(cvs, axis=0), PEER_TOPK, payload=jnp.concatenate(ces, axis=0))
    p = jnp.exp(cs - cs[0:1, :])
    return e, p / jnp.sum(p, axis=0, keepdims=True)


def _post_kernel(tb, x_ref, at_ref, bc_ref, ga_ref, gc_ref, gt1_ref, sh2_ref, sc2_ref, n2_ref,
                 wa_ref, wc_ref, wo_ref, wpq_ref, sk_ref, x1_ref, h2_ref, e_ref, g_ref, qp_ref):
    merged = (ga_ref[...] * jnp.dot(at_ref[...], wa_ref[...], preferred_element_type=F32)
              + gc_ref[...] * jnp.dot(bc_ref[...], wc_ref[...], preferred_element_type=F32))
    x1 = x_ref[...] + gt1_ref[...] * jnp.dot(merged.astype(BF16), wo_ref[...], preferred_element_type=F32)
    x1_ref[...] = x1
    ms = jnp.mean(x1 * x1, axis=-1, keepdims=True)
    h2 = (x1 * lax.rsqrt(ms + EPS) * n2_ref[...]) * (1.0 + sc2_ref[...]) + sh2_ref[...]
    h2_ref[...] = h2
    qp = jnp.dot(h2.astype(BF16), wpq_ref[...], preferred_element_type=F32).astype(BF16)
    for hp in range(2 * PEER_HEADS):
        qp_ref[hp] = qp[:, hp * PEER_HALF:(hp + 1) * PEER_HALF]

    def head(h, carry):
        e, g = _peer_route(qp_ref[2 * h], qp_ref[2 * h + 1], sk_ref[2 * h], sk_ref[2 * h + 1])
        e_ref[h] = e
        g_ref[h] = g
        return carry

    lax.fori_loop(0, PEER_HEADS, head, 0)


def _post(sample, x, attn, bc, ga, gc, gt1, sh2, sc2, norm2, wa, wc, wo, wpq, sk, *, tb, steps_per_batch):
    t = x.shape[0]
    row = lambda w: pl.BlockSpec((tb, w), lambda i: (i, 0))
    if sample:
        mod_spec = pl.BlockSpec((tb, D_MODEL), lambda i: (0, 0))
    else:
        mod_spec = pl.BlockSpec((None, 1, D_MODEL), lambda i: (i // steps_per_batch, 0, 0))
    sel_spec = pl.BlockSpec((PEER_HEADS, PEER_TOPK, tb), lambda i: (0, 0, i))
    return pl.pallas_call(
        functools.partial(_post_kernel, tb),
        grid=(t // tb,),
        in_specs=[row(D_MODEL), row(Q_W), row(D_CONV), row(D_MODEL), row(D_MODEL), mod_spec, mod_spec, mod_spec,
                  _const_spec((1, D_MODEL)), _const_spec(wa.shape), _const_spec(wc.shape), _const_spec(wo.shape),
                  _const_spec(wpq.shape), _const_spec(sk.shape)],
        out_specs=[row(D_MODEL), row(D_MODEL), sel_spec, sel_spec],
        out_shape=[jax.ShapeDtypeStruct((t, D_MODEL), F32), jax.ShapeDtypeStruct((t, D_MODEL), F32),
                   jax.ShapeDtypeStruct((PEER_HEADS, PEER_TOPK, t), jnp.int32),
                   jax.ShapeDtypeStruct((PEER_HEADS, PEER_TOPK, t), F32)],
        scratch_shapes=[pltpu.VMEM((2 * PEER_HEADS, tb, PEER_HALF), BF16)],
        compiler_params=_params(1),
        name="post_sample" if sample else "post_prompt",
    )(x, attn, bc, ga, gc, gt1, sh2, sc2, norm2, wa, wc, wo, wpq, sk)


def _sublane_sums(p):
    sub = lax.broadcasted_iota(jnp.int32, (SUBLANES, LANES), 0)

    def merge(a, b, sh):
        lo = (sub & sh) == 0
        fa = a + pltpu.roll(a, SUBLANES - sh, axis=0)
        fb = b + pltpu.roll(b, sh, axis=0)
        return jnp.where(lo, fa, fb)

    n = [merge(p[i], p[i + 4], 4) for i in range(4)]
    q = [merge(n[0], n[2], 2), merge(n[1], n[3], 2)]
    return merge(q[0], q[1], 1)


def _u_half(word):
    return lax.bitcast_convert_type(word & jnp.int32(-65536), F32)


def _v_half(word):
    return lax.bitcast_convert_type(word << 16, F32)


def _peer_kernel(n, idx_cur, idx_nxt, h_ref, g_ref, uv_hbm, o_ref, buf0, buf1, wbuf, sem):
    i = pl.program_id(0)
    phase_rows = PEER_TOK * PEER_SEL
    bufs = (buf0, buf1)

    def token_rows(s, t):
        return bufs[s].at[pl.ds(t * PEER_SEL, PEER_SEL)]

    def issue_token(idx_ref, off, s, t):
        for k in range(PEER_SEL):
            r = t * PEER_SEL + k
            pltpu.make_async_copy(uv_hbm.at[idx_ref[off + r]], bufs[s].at[r], sem.at[s, t]).start(
                priority=k % PEER_DMA_QUEUES)

    def wait_token(s, t):
        pltpu.make_async_copy(uv_hbm.at[pl.ds(0, PEER_SEL)], token_rows(s, t), sem.at[s, t]).wait()

    @pl.when(i == 0)
    def _():
        def body(r, c):
            pltpu.make_async_copy(uv_hbm.at[idx_cur[r]], buf0.at[r], sem.at[0, r // PEER_SEL]).start()
            return c
        lax.fori_loop(0, phase_rows, body, 0)

    eye = (lax.broadcasted_iota(jnp.int32, (PEER_SEL, PEER_SEL), 0)
           == lax.broadcasted_iota(jnp.int32, (PEER_SEL, PEER_SEL), 1)).astype(F32)
    gcol = lax.dot_general(eye, g_ref[...], (((1,), (1,)), ((), ())), precision=HIGHEST, preferred_element_type=F32)
    for s in range(2):
        for t in range(PEER_TOK):
            wait_token(s, t)
            if s == 0:
                issue_token(idx_cur, phase_rows, 1, t)
            else:
                issue_token(idx_nxt, 0, 0, t)
            tok = s * PEER_TOK + t
            h = h_ref[tok]
            base = t * PEER_SEL
            parts = []
            for k0 in range(0, PEER_SEL, SUBLANES):
                parts.append(_sublane_sums([_u_half(bufs[s][base + k0 + j]) * h for j in range(SUBLANES)]))
            a = jnp.sum(jnp.concatenate(parts, axis=0), axis=1, keepdims=True)
            w = gcol[:, tok:tok + 1] * jax.nn.gelu(a)
            wbuf[...] = jnp.broadcast_to(w, (PEER_SEL, LANES))
            accs = [jnp.zeros((SUBLANES, LANES), F32) for _ in range(4)]
            for k in range(PEER_SEL):
                accs[k % 4] = accs[k % 4] + wbuf[k:k + 1, :] * _v_half(bufs[s][base + k])
            o_ref[tok] = (accs[0] + accs[1]) + (accs[2] + accs[3])

    @pl.when(i == n - 1)
    def _():
        for t in range(PEER_TOK):
            wait_token(0, t)


def _peer(idx_flat, h3, g_tok, uv3):
    t = h3.shape[0]
    step_tok = 2 * PEER_TOK
    n = t // step_tok
    rows = step_tok * PEER_SEL
    tile = pl.BlockSpec((step_tok, SUBLANES, LANES), lambda i: (i, 0, 0))
    return pl.pallas_call(
        functools.partial(_peer_kernel, n),
        grid=(n,),
        in_specs=[pl.BlockSpec((rows,), lambda i: (i,), memory_space=pltpu.SMEM),
                  pl.BlockSpec((rows,), lambda i: (jnp.minimum(i + 1, n - 1),), memory_space=pltpu.SMEM),
                  tile,
                  pl.BlockSpec((step_tok, PEER_SEL), lambda i: (i, 0)),
                  pl.BlockSpec(memory_space=pl.ANY)],
        out_specs=tile,
        out_shape=jax.ShapeDtypeStruct((t, SUBLANES, LANES), F32),
        scratch_shapes=[pltpu.VMEM((PEER_TOK * PEER_SEL, SUBLANES, LANES), jnp.int32),
                        pltpu.VMEM((PEER_TOK * PEER_SEL, SUBLANES, LANES), jnp.int32),
                        pltpu.VMEM((PEER_SEL, LANES), F32),
                        pltpu.SemaphoreType.DMA((2, PEER_TOK))],
        compiler_params=pltpu.CompilerParams(dimension_semantics=("arbitrary",), vmem_limit_bytes=VMEM_LIMIT,
                                             disable_bounds_checks=True),
        name="peer_mix",
    )(idx_flat, idx_flat, h3, g_tok, uv3)


def _gelu_tanh_via_exp(a):
    z = 0.7978845608028654 * (a + 0.044715 * (a * a * a))
    t = 1.0 - 2.0 / (jnp.exp(2.0 * z) + 1.0)
    return 0.5 * a * (1.0 + t)


def _sc_peer(tab, idx, h, g):
    t_all = idx.shape[0]
    per_w = t_all // SC_WORKERS
    n_pass = PEER_SEL // SC_PASS
    n_chunk = D_MODEL // SC_LANES
    quarter = 16 * SC_LANES
    mesh = plsc.VectorSubcoreMesh(core_axis_name="c", subcore_axis_name="s")

    @functools.partial(
        pl.kernel, mesh=mesh,
        out_type=jax.ShapeDtypeStruct((t_all, D_MODEL), F32),
        scratch_types=[pltpu.VMEM((2, PEER_SEL), jnp.int32),
                       pltpu.VMEM((2, SC_PASS, SUBLANES, LANES), jnp.int32),
                       pltpu.VMEM((2, D_MODEL), F32),
                       pltpu.VMEM((2, PEER_SEL), F32),
                       pltpu.VMEM((SC_PASS, SC_LANES), F32),
                       pltpu.VMEM((PEER_SEL,), F32),
                       pltpu.VMEM((D_MODEL,), F32),
                       pltpu.SemaphoreType.DMA((2,)),
                       pltpu.SemaphoreType.DMA((2,))],
        compiler_params=pltpu.CompilerParams(needs_layout_passes=False),
    )
    def k(tab_hbm, idx_hbm, h_hbm, g_hbm, out_hbm, idx_v, rows_v, h_v, g_v, ap_v, w_v, o_v, tsem, rsem):
        wid = lax.axis_index("s") * SC_CORES + lax.axis_index("c")
        lane = lax.iota(jnp.int32, SC_LANES)
        zero = jnp.zeros((SC_LANES,), F32)
        tok0 = wid * per_w

        def token_inputs(tok, slot):
            return (pltpu.make_async_copy(idx_hbm.at[tok], idx_v.at[slot], tsem.at[slot]),
                    pltpu.make_async_copy(h_hbm.at[tok], h_v.at[slot], tsem.at[slot]),
                    pltpu.make_async_copy(g_hbm.at[tok], g_v.at[slot], tsem.at[slot]))

        def gather(slot, q, b):
            ids = idx_v.at[slot].at[pl.ds(q * SC_PASS, SC_PASS)]
            return pltpu.make_async_copy(tab_hbm.at[ids], rows_v.at[b], rsem.at[b])

        def row_chunk(b, kk, jc, j):
            col = jc * quarter + j * SC_LANES
            return rows_v[b, kk, col // LANES, pl.ds(col % LANES, SC_LANES)]

        for c in token_inputs(tok0, 0):
            c.start()
        for c in token_inputs(tok0, 0):
            c.wait()
        gather(0, 0, 0).start()

        def token(ti, carry):
            ts = lax.rem(ti, 2)
            tn = 1 - ts
            tok = tok0 + ti
            nxt = tok0 + jnp.minimum(ti + 1, per_w - 1)
            for c in token_inputs(nxt, tn):
                c.start()
            for j in range(n_chunk):
                o_v[pl.ds(j * SC_LANES, SC_LANES)] = zero
            for q in range(n_pass):
                b = q % 2
                gather(ts, q, b).wait()
                if q + 1 < n_pass:
                    gather(ts, q + 1, 1 - b).start()
                else:
                    for c in token_inputs(nxt, tn):
                        c.wait()
                    gather(tn, 0, 1 - b).start()
                for jc in range(D_MODEL // quarter):
                    hregs = [h_v[ts, pl.ds(jc * quarter + j * SC_LANES, SC_LANES)] for j in range(16)]

                    def dot_rows(kk, jc=jc, hregs=hregs, b=b):
                        part = [None] * 4
                        for j in range(16):
                            p = _u_half(row_chunk(b, kk, jc, j)) * hregs[j]
                            part[j % 4] = p if part[j % 4] is None else part[j % 4] + p
                        acc = (part[0] + part[1]) + (part[2] + part[3])
                        if jc == 0:
                            ap_v[kk, :] = acc
                        else:
                            plsc.addupdate(ap_v.at[kk], acc)

                    plsc.parallel_loop(0, SC_PASS)(dot_rows)
                for m in range(SC_PASS // SC_LANES):
                    rowi = lane + m * SC_LANES
                    a = zero
                    for col in range(SC_LANES):
                        a = a + plsc.load_gather(ap_v, [rowi, jnp.full((SC_LANES,), col, jnp.int32)])
                    off = q * SC_PASS + m * SC_LANES
                    w_v[pl.ds(off, SC_LANES)] = g_v[ts, pl.ds(off, SC_LANES)] * _gelu_tanh_via_exp(a)
                for jc in range(D_MODEL // quarter):
                    def mix_rows(kk, accs, q=q, jc=jc, b=b):
                        ws = plsc.load_gather(w_v, [jnp.full((SC_LANES,), q * SC_PASS, jnp.int32) + kk])
                        return tuple(accs[j] + ws * _v_half(row_chunk(b, kk, jc, j)) for j in range(16))

                    accs = plsc.parallel_loop(0, SC_PASS, carry=tuple(zero for _ in range(16)))(mix_rows)
                    for j in range(16):
                        plsc.addupdate(o_v.at[pl.ds(jc * quarter + j * SC_LANES, SC_LANES)], accs[j])
            pltpu.sync_copy(o_v, out_hbm.at[tok])
            return carry

        lax.fori_loop(0, per_w, token, 0)
        gather(0, 0, 0).wait()

    return k(tab, idx, h, g)


def _final_kernel(x_ref, gt_ref, o_ref, y_ref):
    y_ref[...] = x_ref[...] + gt_ref[...] * o_ref[...]


def _final(sample, x1, gt2, out, *, tb, steps_per_batch):
    t = x1.shape[0]
    row = pl.BlockSpec((tb, D_MODEL), lambda i: (i, 0))
    if sample:
        mod_spec = pl.BlockSpec((tb, D_MODEL), lambda i: (0, 0))
    else:
        mod_spec = pl.BlockSpec((None, 1, D_MODEL), lambda i: (i // steps_per_batch, 0, 0))
    return pl.pallas_call(
        _final_kernel,
        grid=(t // tb,),
        in_specs=[row, mod_spec, row],
        out_specs=row,
        out_shape=jax.ShapeDtypeStruct((t, D_MODEL), F32),
        compiler_params=_params(1),
        name="final_sample" if sample else "final_prompt",
    )(x1, gt2, out)


def _rope_tables(pos):
    half = HEAD_DIM // 2
    inv = jnp.power(jnp.float32(ROPE_THETA), -jnp.arange(half, dtype=F32) / half)
    ang = pos.astype(F32)[:, None] * inv[None, :]
    lane = np.arange(LANES)
    sign = jnp.asarray(np.where((lane % HEAD_DIM) < half, -1.0, 1.0).astype(np.float32))
    cos = jnp.cos(ang)[:, lane % half]
    sin = jnp.sin(ang)[:, lane % half] * sign[None, :]
    return cos, sin


def _dense(sample, x, mods, pos, lw, *, tb, steps_per_batch, batch, seq, win_k=None, win_v=None, conv_state=None):
    sh1, sc1, gt1, sh2, sc2, _ = mods
    cos, sin = _rope_tables(pos)
    if sample:
        cos, sin = cos[:, None, :], sin[:, None, :]
    q, k, v, bc, ga, gc, u_out = _in_proj(sample, x, sh1, sc1, lw["norm1"], lw["w_in"], lw["qn"], lw["kn"], cos, sin,
                                          lw["conv_w"], conv_state, tb=tb, steps_per_batch=steps_per_batch)
    if sample:
        n_new, nb = seq, batch
        q4 = q.reshape(n_new, nb, N_KV_HEADS, GROUP, HEAD_DIM).transpose(1, 2, 3, 0, 4).reshape(
            nb, N_KV_HEADS, GROUP * n_new, HEAD_DIM)
        to_b = lambda a: jnp.pad(a.reshape(n_new, nb, KV_W).transpose(1, 0, 2), ((0, 0), (0, SUBLANES - n_new), (0, 0)))
        sink_mat = jnp.repeat(lw["sinks"].reshape(N_KV_HEADS, GROUP), n_new, axis=1).T
        o4 = _attn_sample(q4, to_b(k), to_b(v), win_k.reshape(nb, WINDOW, KV_W), win_v.reshape(nb, WINDOW, KV_W),
                          sink_mat, n_new)
        attn = o4.reshape(nb, N_KV_HEADS, GROUP, n_new, HEAD_DIM).transpose(3, 0, 1, 2, 4).reshape(n_new * nb, Q_W)
    else:
        attn = _attn_prompt(q, k, v, lw["sinks"], batch, seq)
    x1, h2, e, g = _post(sample, x, attn, bc, ga, gc, gt1, sh2, sc2, lw["norm2"], lw["wa"], lw["wc"], lw["wo"],
                         lw["wpq"], lw["sk"], tb=tb, steps_per_batch=steps_per_batch)
    t = x.shape[0]
    return x1, h2, e.reshape(PEER_SEL, t).T, g.reshape(PEER_SEL, t).T, k, v, u_out


def _tc_peer(lw, idx, h2, g_tok):
    t = idx.shape[0]
    return _peer(idx.reshape(t * PEER_SEL), h2.reshape(t, SUBLANES, LANES), g_tok,
                 lw["uv"]).reshape(t, D_MODEL)


def _prompt_peer(lw, parts):
    unit = 2 * SC_WORKERS
    total = sum(p[0].shape[0] for p in parts)
    left = int(total * SC_TOKEN_SHARE) // unit * unit
    n_sc = [0] * len(parts)
    for i in reversed(range(len(parts))):
        n_sc[i] = min(left, parts[i][0].shape[0])
        left -= n_sc[i]
    sc_out = [_sc_peer(lw["uv"], idx[:n], h2[:n], g_tok[:n]) if n else None
              for (idx, h2, g_tok), n in zip(parts, n_sc)]
    rest = [(idx[n:], h2[n:], g_tok[n:]) for (idx, h2, g_tok), n in zip(parts, n_sc) if n < idx.shape[0]]
    tc_out = _tc_peer(lw, *(jnp.concatenate(c, axis=0) for c in zip(*rest))) if rest else None
    outs, off = [], 0
    for (idx, _, _), n, so in zip(parts, n_sc, sc_out):
        if so is not None:
            outs.append(so)
        m = idx.shape[0] - n
        if m:
            outs.append(tc_out[off:off + m])
            off += m
    return jnp.concatenate(outs, axis=0)


def _pack_uv(u, v):
    ub = lax.bitcast_convert_type(u.astype(BF16), jnp.uint16).astype(jnp.uint32)
    vb = lax.bitcast_convert_type(v.astype(BF16), jnp.uint16).astype(jnp.uint32)
    return lax.bitcast_convert_type((ub << 16) | vb, jnp.int32).reshape(-1, SUBLANES, LANES)


def kernel(x_prompt, x_sample, c_prompt, c_sample, cache_win_k, cache_win_v, state_conv, w_ada, b_ada, norm1, w_in, q_norm, k_norm, attn_sinks, conv_w, w_attn_o, w_conv_o, w_out, norm2, w_peer_q, peer_sub_keys, peer_u, peer_v):
    depth = w_ada.shape[0]
    bp, sp, _ = x_prompt.shape
    bs, ss, _ = x_sample.shape
    assert sp % ATTN_BLOCK == 0 and ss <= SUBLANES and (bs * ss) % (2 * PEER_TOK) == 0
    tb_p = 256 if sp % 256 == 0 else ATTN_BLOCK
    pos_p = jnp.arange(sp)
    pos_s = PAST_LEN + jnp.arange(ss)
    yp = x_prompt.reshape(bp * sp, D_MODEL)
    ys = x_sample.transpose(1, 0, 2).reshape(ss * bs, D_MODEL)
    kp_l, vp_l, cp_l, ks_l, vs_l, cs_l = [], [], [], [], [], []
    for l in range(depth):
        lw = {
            "norm1": norm1[l].reshape(1, D_MODEL), "norm2": norm2[l].reshape(1, D_MODEL),
            "w_in": w_in[l].astype(BF16),
            "qn": jnp.tile(q_norm[l], N_HEADS).reshape(1, Q_W), "kn": jnp.tile(k_norm[l], N_KV_HEADS).reshape(1, KV_W),
            "sinks": attn_sinks[l], "conv_w": conv_w[l],
            "wa": w_attn_o[l].astype(BF16), "wc": w_conv_o[l].astype(BF16), "wo": w_out[l].astype(BF16),
            "wpq": w_peer_q[l].astype(BF16),
            "sk": peer_sub_keys[l].reshape(2 * PEER_HEADS, N_KEYS, PEER_HALF).astype(BF16),
            "uv": _pack_uv(peer_u[l], peer_v[l]),
        }
        c_all = jnp.concatenate([c_prompt, c_sample], axis=0)
        pad = (-c_all.shape[0]) % SUBLANES
        mod = _mod(jnp.pad(c_all, ((0, pad), (0, 0))), w_ada[l], b_ada[l])
        mods_p = tuple(m.reshape(bp, 1, D_MODEL) for m in jnp.split(mod[:bp], 6, axis=-1))
        mods_s = tuple(jnp.split(mod[bp:bp + bs], 6, axis=-1))
        x1, h2, idx, g_tok, ksn, vsn, us = _dense(True, ys, mods_s, pos_s, lw, tb=bs, steps_per_batch=1, batch=bs,
                                                  seq=ss, win_k=cache_win_k[l], win_v=cache_win_v[l],
                                                  conv_state=state_conv[l].transpose(1, 0, 2))
        ys = _final(True, x1, mods_s[5], _tc_peer(lw, idx, h2, g_tok), tb=bs, steps_per_batch=1)
        seqs = [_dense(False, yp[b * sp:(b + 1) * sp], tuple(m[b:b + 1] for m in mods_p), pos_p, lw, tb=tb_p,
                       steps_per_batch=sp // tb_p, batch=1, seq=sp) for b in range(bp)]
        out = _prompt_peer(lw, [(s[2], s[1], s[3]) for s in seqs])
        yp = _final(False, jnp.concatenate([s[0] for s in seqs], axis=0), mods_p[5], out, tb=tb_p,
                    steps_per_batch=sp // tb_p)
        kp_l.append(jnp.stack([s[4][-WINDOW:].reshape(WINDOW, N_KV_HEADS, HEAD_DIM) for s in seqs]))
        vp_l.append(jnp.stack([s[5][-WINDOW:].reshape(WINDOW, N_KV_HEADS, HEAD_DIM) for s in seqs]))
        cp_l.append(jnp.stack([s[6][-(CONV_WIDTH - 1):] for s in seqs]))
        k_new = ksn.reshape(ss, bs, N_KV_HEADS, HEAD_DIM).transpose(1, 0, 2, 3)
        v_new = vsn.reshape(ss, bs, N_KV_HEADS, HEAD_DIM).transpose(1, 0, 2, 3)
        ks_l.append(jnp.concatenate([cache_win_k[l], k_new], axis=1)[:, -WINDOW:])
        vs_l.append(jnp.concatenate([cache_win_v[l], v_new], axis=1)[:, -WINDOW:])
        ext = jnp.concatenate([state_conv[l], us.reshape(ss, bs, D_CONV).transpose(1, 0, 2)], axis=1)
        cs_l.append(ext[:, -(CONV_WIDTH - 1):])
    return (yp.reshape(bp, sp, D_MODEL), ys.reshape(ss, bs, D_MODEL).transpose(1, 0, 2),
            jnp.stack(kp_l), jnp.stack(vp_l), jnp.stack(cp_l), jnp.stack(ks_l), jnp.stack(vs_l), jnp.stack(cs_l))
```

```python
import functools

import numpy as np
import jax
import jax.numpy as jnp
from jax import lax
from jax.experimental import pallas as pl
from jax.experimental.pallas import tpu as pltpu
from jax.experimental.pallas import tpu_sc as plsc

F32 = jnp.float32
BF16 = jnp.bfloat16
HIGHEST = lax.Precision.HIGHEST

D_MODEL = 1024
N_HEADS = 16
HEAD_DIM = 64
N_KV_HEADS = 4
GROUP = N_HEADS // N_KV_HEADS
WINDOW = 128
ATTN_BLOCK = 128
ROPE_THETA = 10000.0
Q_W = N_HEADS * HEAD_DIM
KV_W = N_KV_HEADS * HEAD_DIM
D_CONV = 1024
CONV_WIDTH = 3
PEER_HEADS = 8
N_KEYS = 128
PEER_TOPK = 16
PEER_HALF = 128
PEER_SEL = PEER_HEADS * PEER_TOPK
PAST_LEN = 16384
EPS = 1e-6
NEG = -1e30
IN_SIZES = (Q_W, KV_W, KV_W, D_CONV, D_CONV, D_CONV, D_MODEL, D_MODEL)
IN_OFFS = tuple(int(v) for v in np.cumsum((0,) + IN_SIZES))
D_IN = IN_OFFS[-1]

LANES = 128
SUBLANES = 8
VMEM_LIMIT = 56 * 1024 * 1024
PEER_TOK = 8
PEER_DMA_QUEUES = 2
SC_CORES = 2
SC_SUBCORES = 16
SC_LANES = 16
SC_WORKERS = SC_CORES * SC_SUBCORES
SC_PASS = 32
SC_TOKEN_SHARE = 0.62
PEER_PAIRS = tuple((i, j) for i in range(PEER_TOPK) for j in range(PEER_TOPK) if (i + 1) * (j + 1) <= PEER_TOPK)


def _const_spec(shape):
    nd = len(shape)
    return pl.BlockSpec(shape, lambda *_: (0,) * nd, pipeline_mode=pl.Buffered(1))


def _params(n_grid):
    return pltpu.CompilerParams(dimension_semantics=("arbitrary",) * n_grid, vmem_limit_bytes=VMEM_LIMIT)


def _mod_kernel(c_ref, w_ref, b_ref, o_ref):
    c = c_ref[...]
    s = c * jax.nn.sigmoid(c)
    o_ref[...] = jnp.dot(s, w_ref[...], precision=HIGHEST, preferred_element_type=F32) + b_ref[...]


def _mod(c, w_ada, b_ada):
    rows, n = c.shape[0], w_ada.shape[1]
    tn = 512
    return pl.pallas_call(
        _mod_kernel,
        grid=(n // tn,),
        in_specs=[pl.BlockSpec((rows, D_MODEL), lambda j: (0, 0)),
                  pl.BlockSpec((D_MODEL, tn), lambda j: (0, j)),
                  pl.BlockSpec((1, tn), lambda j: (0, j))],
        out_specs=pl.BlockSpec((rows, tn), lambda j: (0, j)),
        out_shape=jax.ShapeDtypeStruct((rows, n), F32),
        compiler_params=_params(1),
        name="mod",
    )(c, w_ada, b_ada.reshape(1, n))


def _dot_hi_lo(x, m01):
    hi = x.astype(BF16)
    lo = (x - hi.astype(F32)).astype(BF16)
    m = m01.astype(BF16)
    return jnp.dot(hi, m, preferred_element_type=F32) + jnp.dot(lo, m, preferred_element_type=F32)


def _head_norm(z, g_ref, gt_ref, w):
    ss = _dot_hi_lo(z * z, g_ref[...])
    ssb = _dot_hi_lo(ss, gt_ref[...])
    return z * lax.rsqrt(ssb * (1.0 / HEAD_DIM) + EPS) * w


def _rope(x, cos, sin_signed):
    w = x.shape[1]
    lane = lax.broadcasted_iota(jnp.int32, x.shape, 1)
    fwd = pltpu.roll(x, w - HEAD_DIM // 2, axis=1)
    bwd = pltpu.roll(x, HEAD_DIM // 2, axis=1)
    partner = jnp.where((lane % HEAD_DIM) < HEAD_DIM // 2, fwd, bwd)
    reps = w // LANES
    return x * jnp.tile(cos, (1, reps)) + partner * jnp.tile(sin_signed, (1, reps))


def _in_kernel(sample, tb, steps_per_batch, *refs):
    if sample:
        (x_ref, sh_ref, sc_ref, n1_ref, w_ref, qn_ref, kn_ref, cos_ref, sin_ref, cw_ref,
         gq_ref, gqt_ref, gk_ref, gkt_ref, st_ref,
         q_ref, k_ref, v_ref, bc_ref, ga_ref, gc_ref, u_ref, prev_ref) = refs
    else:
        (x_ref, sh_ref, sc_ref, n1_ref, w_ref, qn_ref, kn_ref, cos_ref, sin_ref, cw_ref,
         gq_ref, gqt_ref, gk_ref, gkt_ref,
         q_ref, k_ref, v_ref, bc_ref, ga_ref, gc_ref, u_ref, ext_ref) = refs
    i = pl.program_id(0)
    x = x_ref[...]
    ms = jnp.mean(x * x, axis=-1, keepdims=True)
    h = (x * lax.rsqrt(ms + EPS) * n1_ref[...]) * (1.0 + sc_ref[...]) + sh_ref[...]
    hb = h.astype(BF16)

    def seg(n):
        return jnp.dot(hb, w_ref[:, IN_OFFS[n]:IN_OFFS[n + 1]], preferred_element_type=F32)

    cos, sin = cos_ref[...], sin_ref[...]
    q = _rope(_head_norm(seg(0), gq_ref, gqt_ref, qn_ref[...]), cos, sin)
    q_ref[...] = (q * (HEAD_DIM ** -0.5)).astype(BF16)
    k_ref[...] = _rope(_head_norm(seg(1), gk_ref, gkt_ref, kn_ref[...]), cos, sin)
    v_ref[...] = seg(2)
    u = seg(4) * seg(5)
    w0, w1, w2 = cw_ref[0:1, :], cw_ref[1:2, :], cw_ref[2:3, :]
    if sample:
        @pl.when(i == 0)
        def _():
            prev_ref[...] = st_ref[...]
        p2, p1 = prev_ref[0], prev_ref[1]
        conv = p2 * w0 + p1 * w1 + u * w2
        prev_ref[0] = p1
        prev_ref[1] = u
        u_ref[...] = u
    else:
        @pl.when(i % steps_per_batch == 0)
        def _():
            ext_ref[0:SUBLANES, :] = jnp.zeros((SUBLANES, D_CONV), F32)

        @pl.when(i % steps_per_batch != 0)
        def _():
            ext_ref[0:SUBLANES, :] = ext_ref[tb:tb + SUBLANES, :]
        ext_ref[SUBLANES:tb + SUBLANES, :] = u
        conv = ext_ref[SUBLANES - 2:tb + SUBLANES - 2, :] * w0 + ext_ref[SUBLANES - 1:tb + SUBLANES - 1, :] * w1 + u * w2
        u_ref[...] = u[tb - SUBLANES:tb, :]
    bc_ref[...] = (seg(3) * conv).astype(BF16)
    ga_ref[...] = jax.nn.sigmoid(seg(6))
    gc_ref[...] = jax.nn.sigmoid(seg(7))


def _group_mats(width):
    lane = np.arange(width)
    g = (lane[:, None] // HEAD_DIM == np.arange(LANES)[None, :]).astype(np.float32)
    return jnp.asarray(g), jnp.asarray(g.T)


def _in_proj(sample, x, sh, sc, norm1, w_in_bf, qn, kn, cos, sin, conv_w, state=None, *, tb, steps_per_batch):
    t = x.shape[0]
    steps = t // tb
    gq, gqt = _group_mats(Q_W)
    gk, gkt = _group_mats(KV_W)
    row = lambda w: pl.BlockSpec((tb, w), lambda i: (i, 0))
    if sample:
        mod_spec = pl.BlockSpec((tb, D_MODEL), lambda i: (0, 0))
        rope_spec = pl.BlockSpec((None, 1, LANES), lambda i: (i, 0, 0))
        u_spec, u_rows = row(D_CONV), t
        scratch = [pltpu.VMEM((2, tb, D_CONV), F32)]
    else:
        mod_spec = pl.BlockSpec((None, 1, D_MODEL), lambda i: (i // steps_per_batch, 0, 0))
        rope_spec = pl.BlockSpec((tb, LANES), lambda i: (i % steps_per_batch, 0))
        u_spec, u_rows = pl.BlockSpec((SUBLANES, D_CONV), lambda i: (i, 0)), steps * SUBLANES
        scratch = [pltpu.VMEM((tb + SUBLANES, D_CONV), F32)]
    in_specs = [row(D_MODEL), mod_spec, mod_spec, _const_spec((1, D_MODEL)), _const_spec((D_MODEL, D_IN)),
                _const_spec((1, Q_W)), _const_spec((1, KV_W)), rope_spec, rope_spec, _const_spec((CONV_WIDTH, D_CONV)),
                _const_spec(gq.shape), _const_spec(gqt.shape), _const_spec(gk.shape), _const_spec(gkt.shape)]
    args = [x, sh, sc, norm1, w_in_bf, qn, kn, cos, sin, conv_w, gq, gqt, gk, gkt]
    if sample:
        in_specs.append(_const_spec(state.shape))
        args.append(state)
    return pl.pallas_call(
        functools.partial(_in_kernel, sample, tb, steps_per_batch),
        grid=(steps,),
        in_specs=in_specs,
        out_specs=[row(Q_W), row(KV_W), row(KV_W), row(D_CONV), row(D_MODEL), row(D_MODEL), u_spec],
        out_shape=[jax.ShapeDtypeStruct((t, Q_W), BF16), jax.ShapeDtypeStruct((t, KV_W), F32),
                   jax.ShapeDtypeStruct((t, KV_W), F32), jax.ShapeDtypeStruct((t, D_CONV), BF16),
                   jax.ShapeDtypeStruct((t, D_MODEL), F32), jax.ShapeDtypeStruct((t, D_MODEL), F32),
                   jax.ShapeDtypeStruct((u_rows, D_CONV), F32)],
        scratch_shapes=scratch,
        compiler_params=_params(1),
        name="in_proj_sample" if sample else "in_proj_prompt",
    )(*args)


def _sink_softmax_pv(q, kg, vg, valid, sink):
    s = lax.dot_general(q, kg, (((1,), (1,)), ((), ())), preferred_element_type=F32)
    s = jnp.where(valid, s, NEG)
    m = jnp.maximum(jnp.max(s, axis=1, keepdims=True), sink)
    p = jnp.exp(s - m)
    denom = jnp.sum(p, axis=1, keepdims=True) + jnp.exp(sink - m)
    return jnp.dot(p.astype(BF16), vg, preferred_element_type=F32) / denom


def _attn_prompt_kernel(sink_ref, q_ref, kc_ref, kp_ref, vc_ref, vp_ref, o_ref):
    j = pl.program_id(1)
    q = q_ref[...]
    kb = jnp.concatenate([kp_ref[...], kc_ref[...]], axis=0).astype(BF16)
    vb = jnp.concatenate([vp_ref[...], vc_ref[...]], axis=0).astype(BF16)
    qi = lax.broadcasted_iota(jnp.int32, (ATTN_BLOCK, 2 * ATTN_BLOCK), 0)
    kj = lax.broadcasted_iota(jnp.int32, (ATTN_BLOCK, 2 * ATTN_BLOCK), 1)
    valid = (kj >= qi) & (kj <= qi + WINDOW) & ((kj >= ATTN_BLOCK) | (j > 0))
    outs = []
    for h in range(N_HEADS):
        g = h // GROUP
        kg = kb[:, g * HEAD_DIM:(g + 1) * HEAD_DIM]
        vg = vb[:, g * HEAD_DIM:(g + 1) * HEAD_DIM]
        outs.append(_sink_softmax_pv(q[:, h * HEAD_DIM:(h + 1) * HEAD_DIM], kg, vg, valid, sink_ref[h]))
    o_ref[...] = jnp.concatenate(outs, axis=1).astype(BF16)


def _attn_prompt(q, k, v, sinks, batch, seq):
    nb = seq // ATTN_BLOCK
    cur = lambda w: pl.BlockSpec((ATTN_BLOCK, w), lambda b, j: (b * nb + j, 0))
    prev = lambda w: pl.BlockSpec((ATTN_BLOCK, w), lambda b, j: (b * nb + jnp.maximum(j - 1, 0), 0))
    return pl.pallas_call(
        _attn_prompt_kernel,
        grid=(batch, nb),
        in_specs=[pl.BlockSpec(memory_space=pltpu.SMEM), cur(Q_W), cur(KV_W), prev(KV_W), cur(KV_W), prev(KV_W)],
        out_specs=cur(Q_W),
        out_shape=jax.ShapeDtypeStruct((batch * seq, Q_W), BF16),
        compiler_params=_params(2),
        name="attn_prompt",
    )(sinks, q, k, k, v, v)


def _attn_sample_kernel(bb, n_new, sink_ref, q_ref, kn_ref, vn_ref, kc_ref, vc_ref, o_ref):
    rows = GROUP * n_new
    s_len = 2 * WINDOW
    ri = lax.broadcasted_iota(jnp.int32, (rows, s_len), 0)
    kj = lax.broadcasted_iota(jnp.int32, (rows, s_len), 1)
    t = ri % n_new
    valid = ((kj < WINDOW) & (kj >= t)) | ((kj >= WINDOW) & (kj - WINDOW <= t))
    pad = jnp.zeros((WINDOW - SUBLANES, KV_W), F32)
    for b in range(bb):
        kb = jnp.concatenate([kc_ref[b], kn_ref[b], pad], axis=0).astype(BF16)
        vb = jnp.concatenate([vc_ref[b], vn_ref[b], pad], axis=0).astype(BF16)
        for g in range(N_KV_HEADS):
            kg = kb[:, g * HEAD_DIM:(g + 1) * HEAD_DIM]
            vg = vb[:, g * HEAD_DIM:(g + 1) * HEAD_DIM]
            o_ref[b, g] = _sink_softmax_pv(q_ref[b, g], kg, vg, valid, sink_ref[:, g:g + 1]).astype(BF16)


def _attn_sample(q4, k_new, v_new, cache_k, cache_v, sink_mat, n_new):
    nbatch = q4.shape[0]
    bb = 8 if nbatch % 8 == 0 else nbatch
    rows = GROUP * n_new
    b3 = lambda r, w: pl.BlockSpec((bb, r, w), lambda i: (i, 0, 0))
    qspec = pl.BlockSpec((bb, N_KV_HEADS, rows, HEAD_DIM), lambda i: (i, 0, 0, 0))
    return pl.pallas_call(
        functools.partial(_attn_sample_kernel, bb, n_new),
        grid=(nbatch // bb,),
        in_specs=[_const_spec(sink_mat.shape), qspec, b3(SUBLANES, KV_W), b3(SUBLANES, KV_W),
                  b3(WINDOW, KV_W), b3(WINDOW, KV_W)],
        out_specs=qspec,
        out_shape=jax.ShapeDtypeStruct(q4.shape, BF16),
        compiler_params=_params(1),
        name="attn_sample",
    )(sink_mat, q4, k_new, v_new, cache_k, cache_v)


def _topk_rows(s, k, payload=None):
    r = s.shape[0]
    rows = lax.broadcasted_iota(jnp.int32, s.shape, 0)
    vals, sel = [], []
    for _ in range(k):
        m = jnp.max(s, axis=0, keepdims=True)
        ix = jnp.min(jnp.where(s == m, rows, r), axis=0, keepdims=True)
        hit = rows == ix
        vals.append(m)
        sel.append(ix if payload is None else jnp.max(jnp.where(hit, payload, -1), axis=0, keepdims=True))
        s = jnp.where(hit, -jnp.inf, s)
    return jnp.concatenate(vals, axis=0), jnp.concatenate(sel, axis=0)


def _peer_route(q0, q1, k0, k1):
    nt = (((1,), (1,)), ((), ()))
    s0 = lax.dot_general(k0, q0, nt, preferred_element_type=F32)
    s1 = lax.dot_general(k1, q1, nt, preferred_element_type=F32)
    sv0, si0 = _topk_rows(s0, PEER_TOPK)
    sv1, si1 = _topk_rows(s1, PEER_TOPK)
    tb = s0.shape[1]
    row8 = lax.broadcasted_iota(jnp.int32, (SUBLANES, tb), 0)
    n_grp = -(-len(PEER_PAIRS) // SUBLANES)
    cvs, ces = [], []
    for gi in range(n_grp):
        cv = jnp.full((SUBLANES, tb), -jnp.inf, F32)
        ce = jnp.zeros((SUBLANES, tb), jnp.int32)
        for r in range(SUBLANES):
            c = gi * SUBLANES + r
            if c < len(PEER_PAIRS):
                i, j = PEER_PAIRS[c]
                cv = jnp.where(row8 == r, sv0[i:i + 1, :] + sv1[j:j + 1, :], cv)
                ce = jnp.where(row8 == r, si0[i:i + 1, :] * N_KEYS + si1[j:j + 1, :], ce)
        cvs.append(cv)
        ces.append(ce)
    cs, e = _topk_rows(jnp.concatenate(cvs, axis=0), PEER_TOPK, payload=jnp.concatenate(ces, axis=0))
    p = jnp.exp(cs - cs[0:1, :])
    return e, p / jnp.sum(p, axis=0, keepdims=True)


def _post_kernel(tb, x_ref, at_ref, bc_ref, ga_ref, gc_ref, gt1_ref, sh2_ref, sc2_ref, n2_ref,
                 wa_ref, wc_ref, wo_ref, wpq_ref, sk_ref, x1_ref, h2_ref, e_ref, g_ref, qp_ref):
    merged = (ga_ref[...] * jnp.dot(at_ref[...], wa_ref[...], preferred_element_type=F32)
              + gc_ref[...] * jnp.dot(bc_ref[...], wc_ref[...], preferred_element_type=F32))
    x1 = x_ref[...] + gt1_ref[...] * jnp.dot(merged.astype(BF16), wo_ref[...], preferred_element_type=F32)
    x1_ref[...] = x1
    ms = jnp.mean(x1 * x1, axis=-1, keepdims=True)
    h2 = (x1 * lax.rsqrt(ms + EPS) * n2_ref[...]) * (1.0 + sc2_ref[...]) + sh2_ref[...]
    h2_ref[...] = h2
    qp = jnp.dot(h2.astype(BF16), wpq_ref[...], preferred_element_type=F32).astype(BF16)
    for hp in range(2 * PEER_HEADS):
        qp_ref[hp] = qp[:, hp * PEER_HALF:(hp + 1) * PEER_HALF]

    def head(h, carry):
        e, g = _peer_route(qp_ref[2 * h], qp_ref[2 * h + 1], sk_ref[2 * h], sk_ref[2 * h + 1])
        e_ref[h] = e
        g_ref[h] = g
        return carry

    lax.fori_loop(0, PEER_HEADS, head, 0)


def _post(sample, x, attn, bc, ga, gc, gt1, sh2, sc2, norm2, wa, wc, wo, wpq, sk, *, tb, steps_per_batch):
    t = x.shape[0]
    row = lambda w: pl.BlockSpec((tb, w), lambda i: (i, 0))
    if sample:
        mod_spec = pl.BlockSpec((tb, D_MODEL), lambda i: (0, 0))
    else:
        mod_spec = pl.BlockSpec((None, 1, D_MODEL), lambda i: (i // steps_per_batch, 0, 0))
    sel_spec = pl.BlockSpec((PEER_HEADS, PEER_TOPK, tb), lambda i: (0, 0, i))
    return pl.pallas_call(
        functools.partial(_post_kernel, tb),
        grid=(t // tb,),
        in_specs=[row(D_MODEL), row(Q_W), row(D_CONV), row(D_MODEL), row(D_MODEL), mod_spec, mod_spec, mod_spec,
                  _const_spec((1, D_MODEL)), _const_spec(wa.shape), _const_spec(wc.shape), _const_spec(wo.shape),
                  _const_spec(wpq.shape), _const_spec(sk.shape)],
        out_specs=[row(D_MODEL), row(D_MODEL), sel_spec, sel_spec],
        out_shape=[jax.ShapeDtypeStruct((t, D_MODEL), F32), jax.ShapeDtypeStruct((t, D_MODEL), F32),
                   jax.ShapeDtypeStruct((PEER_HEADS, PEER_TOPK, t), jnp.int32),
                   jax.ShapeDtypeStruct((PEER_HEADS, PEER_TOPK, t), F32)],
        scratch_shapes=[pltpu.VMEM((2 * PEER_HEADS, tb, PEER_HALF), BF16)],
        compiler_params=_params(1),
        name="post_sample" if sample else "post_prompt",
    )(x, attn, bc, ga, gc, gt1, sh2, sc2, norm2, wa, wc, wo, wpq, sk)


def _sublane_sums(p):
    sub = lax.broadcasted_iota(jnp.int32, (SUBLANES, LANES), 0)

    def merge(a, b, sh):
        lo = (sub & sh) == 0
        fa = a + pltpu.roll(a, SUBLANES - sh, axis=0)
        fb = b + pltpu.roll(b, sh, axis=0)
        return jnp.where(lo, fa, fb)

    n = [merge(p[i], p[i + 4], 4) for i in range(4)]
    q = [merge(n[0], n[2], 2), merge(n[1], n[3], 2)]
    return merge(q[0], q[1], 1)


def _u_half(word):
    return lax.bitcast_convert_type(word & jnp.int32(-65536), F32)


def _v_half(word):
    return lax.bitcast_convert_type(word << 16, F32)


def _peer_kernel(n, idx_cur, idx_nxt, h_ref, g_ref, uv_hbm, o_ref, buf0, buf1, wbuf, sem):
    i = pl.program_id(0)
    phase_rows = PEER_TOK * PEER_SEL
    bufs = (buf0, buf1)

    def token_rows(s, t):
        return bufs[s].at[pl.ds(t * PEER_SEL, PEER_SEL)]

    def issue_token(idx_ref, off, s, t):
        for k in range(PEER_SEL):
            r = t * PEER_SEL + k
            pltpu.make_async_copy(uv_hbm.at[idx_ref[off + r]], bufs[s].at[r], sem.at[s, t]).start(
                priority=k % PEER_DMA_QUEUES)

    def wait_token(s, t):
        pltpu.make_async_copy(uv_hbm.at[pl.ds(0, PEER_SEL)], token_rows(s, t), sem.at[s, t]).wait()

    @pl.when(i == 0)
    def _():
        def body(r, c):
            pltpu.make_async_copy(uv_hbm.at[idx_cur[r]], buf0.at[r], sem.at[0, r // PEER_SEL]).start()
            return c
        lax.fori_loop(0, phase_rows, body, 0)

    eye = (lax.broadcasted_iota(jnp.int32, (PEER_SEL, PEER_SEL), 0)
           == lax.broadcasted_iota(jnp.int32, (PEER_SEL, PEER_SEL), 1)).astype(F32)
    gcol = lax.dot_general(eye, g_ref[...], (((1,), (1,)), ((), ())), precision=HIGHEST, preferred_element_type=F32)
    for s in range(2):
        for t in range(PEER_TOK):
            wait_token(s, t)
            if s == 0:
                issue_token(idx_cur, phase_rows, 1, t)
            else:
                issue_token(idx_nxt, 0, 0, t)
            tok = s * PEER_TOK + t
            h = h_ref[tok]
            base = t * PEER_SEL
            parts = []
            for k0 in range(0, PEER_SEL, SUBLANES):
                parts.append(_sublane_sums([_u_half(bufs[s][base + k0 + j]) * h for j in range(SUBLANES)]))
            a = jnp.sum(jnp.concatenate(parts, axis=0), axis=1, keepdims=True)
            w = gcol[:, tok:tok + 1] * jax.nn.gelu(a)
            wbuf[...] = jnp.broadcast_to(w, (PEER_SEL, LANES))
            accs = [jnp.zeros((SUBLANES, LANES), F32) for _ in range(4)]
            for k in range(PEER_SEL):
                accs[k % 4] = accs[k % 4] + wbuf[k:k + 1, :] * _v_half(bufs[s][base + k])
            o_ref[tok] = (accs[0] + accs[1]) + (accs[2] + accs[3])

    @pl.when(i == n - 1)
    def _():
        for t in range(PEER_TOK):
            wait_token(0, t)


def _peer(idx_flat, h3, g_tok, uv3):
    t = h3.shape[0]
    step_tok = 2 * PEER_TOK
    n = t // step_tok
    rows = step_tok * PEER_SEL
    tile = pl.BlockSpec((step_tok, SUBLANES, LANES), lambda i: (i, 0, 0))
    return pl.pallas_call(
        functools.partial(_peer_kernel, n),
        grid=(n,),
        in_specs=[pl.BlockSpec((rows,), lambda i: (i,), memory_space=pltpu.SMEM),
                  pl.BlockSpec((rows,), lambda i: (jnp.minimum(i + 1, n - 1),), memory_space=pltpu.SMEM),
                  tile,
                  pl.BlockSpec((step_tok, PEER_SEL), lambda i: (i, 0)),
                  pl.BlockSpec(memory_space=pl.ANY)],
        out_specs=tile,
        out_shape=jax.ShapeDtypeStruct((t, SUBLANES, LANES), F32),
        scratch_shapes=[pltpu.VMEM((PEER_TOK * PEER_SEL, SUBLANES, LANES), jnp.int32),
                        pltpu.VMEM((PEER_TOK * PEER_SEL, SUBLANES, LANES), jnp.int32),
                        pltpu.VMEM((PEER_SEL, LANES), F32),
                        pltpu.SemaphoreType.DMA((2, PEER_TOK))],
        compiler_params=pltpu.CompilerParams(dimension_semantics=("arbitrary",), vmem_limit_bytes=VMEM_LIMIT,
                                             disable_bounds_checks=True),
        name="peer_mix",
    )(idx_flat, idx_flat, h3, g_tok, uv3)


def _gelu_tanh_via_exp(a):
    z = 0.7978845608028654 * (a + 0.044715 * (a * a * a))
    t = 1.0 - 2.0 / (jnp.exp(2.0 * z) + 1.0)
    return 0.5 * a * (1.0 + t)


def _sc_peer(tab, idx, h, g):
    t_all = idx.shape[0]
    per_w = t_all // SC_WORKERS
    n_pass = PEER_SEL // SC_PASS
    n_chunk = D_MODEL // SC_LANES
    quarter = 16 * SC_LANES
    mesh = plsc.VectorSubcoreMesh(core_axis_name="c", subcore_axis_name="s")

    @functools.partial(
        pl.kernel, mesh=mesh,
        out_type=jax.ShapeDtypeStruct((t_all, D_MODEL), F32),
        scratch_types=[pltpu.VMEM((2, PEER_SEL), jnp.int32),
                       pltpu.VMEM((2, SC_PASS, SUBLANES, LANES), jnp.int32),
                       pltpu.VMEM((2, D_MODEL), F32),
                       pltpu.VMEM((2, PEER_SEL), F32),
                       pltpu.VMEM((SC_PASS, SC_LANES), F32),
                       pltpu.VMEM((PEER_SEL,), F32),
                       pltpu.VMEM((D_MODEL,), F32),
                       pltpu.SemaphoreType.DMA((2,)),
                       pltpu.SemaphoreType.DMA((2,))],
        compiler_params=pltpu.CompilerParams(needs_layout_passes=False),
    )
    def k(tab_hbm, idx_hbm, h_hbm, g_hbm, out_hbm, idx_v, rows_v, h_v, g_v, ap_v, w_v, o_v, tsem, rsem):
        wid = lax.axis_index("s") * SC_CORES + lax.axis_index("c")
        lane = lax.iota(jnp.int32, SC_LANES)
        zero = jnp.zeros((SC_LANES,), F32)
        tok0 = wid * per_w

        def token_inputs(tok, slot):
            return (pltpu.make_async_copy(idx_hbm.at[tok], idx_v.at[slot], tsem.at[slot]),
                    pltpu.make_async_copy(h_hbm.at[tok], h_v.at[slot], tsem.at[slot]),
                    pltpu.make_async_copy(g_hbm.at[tok], g_v.at[slot], tsem.at[slot]))

        def gather(slot, q, b):
            ids = idx_v.at[slot].at[pl.ds(q * SC_PASS, SC_PASS)]
            return pltpu.make_async_copy(tab_hbm.at[ids], rows_v.at[b], rsem.at[b])

        def row_chunk(b, kk, jc, j):
            col = jc * quarter + j * SC_LANES
            return rows_v[b, kk, col // LANES, pl.ds(col % LANES, SC_LANES)]

        for c in token_inputs(tok0, 0):
            c.start()
        for c in token_inputs(tok0, 0):
            c.wait()
        gather(0, 0, 0).start()

        def token(ti, carry):
            ts = lax.rem(ti, 2)
            tn = 1 - ts
            tok = tok0 + ti
            nxt = tok0 + jnp.minimum(ti + 1, per_w - 1)
            for c in token_inputs(nxt, tn):
                c.start()
            for j in range(n_chunk):
                o_v[pl.ds(j * SC_LANES, SC_LANES)] = zero
            for q in range(n_pass):
                b = q % 2
                gather(ts, q, b).wait()
                if q + 1 < n_pass:
                    gather(ts, q + 1, 1 - b).start()
                else:
                    for c in token_inputs(nxt, tn):
                        c.wait()
                    gather(tn, 0, 1 - b).start()
                for jc in range(D_MODEL // quarter):
                    hregs = [h_v[ts, pl.ds(jc * quarter + j * SC_LANES, SC_LANES)] for j in range(16)]

                    def dot_rows(kk, jc=jc, hregs=hregs, b=b):
                        part = [None] * 4
                        for j in range(16):
                            p = _u_half(row_chunk(b, kk, jc, j)) * hregs[j]
                            part[j % 4] = p if part[j % 4] is None else part[j % 4] + p
                        acc = (part[0] + part[1]) + (part[2] + part[3])
                        if jc == 0:
                            ap_v[kk, :] = acc
                        else:
                            plsc.addupdate(ap_v.at[kk], acc)

                    plsc.parallel_loop(0, SC_PASS)(dot_rows)
                for m in range(SC_PASS // SC_LANES):
                    rowi = lane + m * SC_LANES
                    a = zero
                    for col in range(SC_LANES):
                        a = a + plsc.load_gather(ap_v, [rowi, jnp.full((SC_LANES,), col, jnp.int32)])
                    off = q * SC_PASS + m * SC_LANES
                    w_v[pl.ds(off, SC_LANES)] = g_v[ts, pl.ds(off, SC_LANES)] * _gelu_tanh_via_exp(a)
                for jc in range(D_MODEL // quarter):
                    def mix_rows(kk, accs, q=q, jc=jc, b=b):
                        ws = plsc.load_gather(w_v, [jnp.full((SC_LANES,), q * SC_PASS, jnp.int32) + kk])
                        return tuple(accs[j] + ws * _v_half(row_chunk(b, kk, jc, j)) for j in range(16))

                    accs = plsc.parallel_loop(0, SC_PASS, carry=tuple(zero for _ in range(16)))(mix_rows)
                    for j in range(16):
                        plsc.addupdate(o_v.at[pl.ds(jc * quarter + j * SC_LANES, SC_LANES)], accs[j])
            pltpu.sync_copy(o_v, out_hbm.at[tok])
            return carry

        lax.fori_loop(0, per_w, token, 0)
        gather(0, 0, 0).wait()

    return k(tab, idx, h, g)


def _final_kernel(x_ref, gt_ref, o_ref, y_ref):
    y_ref[...] = x_ref[...] + gt_ref[...] * o_ref[...]


def _final(sample, x1, gt2, out, *, tb, steps_per_batch):
    t = x1.shape[0]
    row = pl.BlockSpec((tb, D_MODEL), lambda i: (i, 0))
    if sample:
        mod_spec = pl.BlockSpec((tb, D_MODEL), lambda i: (0, 0))
    else:
        mod_spec = pl.BlockSpec((None, 1, D_MODEL), lambda i: (i // steps_per_batch, 0, 0))
    return pl.pallas_call(
        _final_kernel,
        grid=(t // tb,),
        in_specs=[row, mod_spec, row],
        out_specs=row,
        out_shape=jax.ShapeDtypeStruct((t, D_MODEL), F32),
        compiler_params=_params(1),
        name="final_sample" if sample else "final_prompt",
    )(x1, gt2, out)


def _rope_tables(pos):
    half = HEAD_DIM // 2
    inv = jnp.power(jnp.float32(ROPE_THETA), -jnp.arange(half, dtype=F32) / half)
    ang = pos.astype(F32)[:, None] * inv[None, :]
    lane = np.arange(LANES)
    sign = jnp.asarray(np.where((lane % HEAD_DIM) < half, -1.0, 1.0).astype(np.float32))
    cos = jnp.cos(ang)[:, lane % half]
    sin = jnp.sin(ang)[:, lane % half] * sign[None, :]
    return cos, sin


def _dense(sample, x, mods, pos, lw, *, tb, steps_per_batch, batch, seq, win_k=None, win_v=None, conv_state=None):
    sh1, sc1, gt1, sh2, sc2, _ = mods
    cos, sin = _rope_tables(pos)
    if sample:
        cos, sin = cos[:, None, :], sin[:, None, :]
    q, k, v, bc, ga, gc, u_out = _in_proj(sample, x, sh1, sc1, lw["norm1"], lw["w_in"], lw["qn"], lw["kn"], cos, sin,
                                          lw["conv_w"], conv_state, tb=tb, steps_per_batch=steps_per_batch)
    if sample:
        n_new, nb = seq, batch
        q4 = q.reshape(n_new, nb, N_KV_HEADS, GROUP, HEAD_DIM).transpose(1, 2, 3, 0, 4).reshape(
            nb, N_KV_HEADS, GROUP * n_new, HEAD_DIM)
        to_b = lambda a: jnp.pad(a.reshape(n_new, nb, KV_W).transpose(1, 0, 2), ((0, 0), (0, SUBLANES - n_new), (0, 0)))
        sink_mat = jnp.repeat(lw["sinks"].reshape(N_KV_HEADS, GROUP), n_new, axis=1).T
        o4 = _attn_sample(q4, to_b(k), to_b(v), win_k.reshape(nb, WINDOW, KV_W), win_v.reshape(nb, WINDOW, KV_W),
                          sink_mat, n_new)
        attn = o4.reshape(nb, N_KV_HEADS, GROUP, n_new, HEAD_DIM).transpose(3, 0, 1, 2, 4).reshape(n_new * nb, Q_W)
    else:
        attn = _attn_prompt(q, k, v, lw["sinks"], batch, seq)
    x1, h2, e, g = _post(sample, x, attn, bc, ga, gc, gt1, sh2, sc2, lw["norm2"], lw["wa"], lw["wc"], lw["wo"],
                         lw["wpq"], lw["sk"], tb=tb, steps_per_batch=steps_per_batch)
    t = x.shape[0]
    return x1, h2, e.reshape(PEER_SEL, t).T, g.reshape(PEER_SEL, t).T, k, v, u_out


def _tc_peer(lw, idx, h2, g_tok):
    t = idx.shape[0]
    return _peer(idx.reshape(t * PEER_SEL), h2.reshape(t, SUBLANES, LANES), g_tok,
                 lw["uv"]).reshape(t, D_MODEL)


def _prompt_peer(lw, parts):
    unit = 2 * SC_WORKERS
    total = sum(p[0].shape[0] for p in parts)
    left = int(total * SC_TOKEN_SHARE) // unit * unit
    n_sc = [0] * len(parts)
    for i in reversed(range(len(parts))):
        n_sc[i] = min(left, parts[i][0].shape[0])
        left -= n_sc[i]
    sc_out = [_sc_peer(lw["uv"], idx[:n], h2[:n], g_tok[:n]) if n else None
              for (idx, h2, g_tok), n in zip(parts, n_sc)]
    rest = [(idx[n:], h2[n:], g_tok[n:]) for (idx, h2, g_tok), n in zip(parts, n_sc) if n < idx.shape[0]]
    tc_out = _tc_peer(lw, *(jnp.concatenate(c, axis=0) for c in zip(*rest))) if rest else None
    outs, off = [], 0
    for (idx, _, _), n, so in zip(parts, n_sc, sc_out):
        if so is not None:
            outs.append(so)
        m = idx.shape[0] - n
        if m:
            outs.append(tc_out[off:off + m])
            off += m
    return jnp.concatenate(outs, axis=0)


def _pack_uv(u, v):
    ub = lax.bitcast_convert_type(u.astype(BF16), jnp.uint16).astype(jnp.uint32)
    vb = lax.bitcast_convert_type(v.astype(BF16), jnp.uint16).astype(jnp.uint32)
    return lax.bitcast_convert_type((ub << 16) | vb, jnp.int32).reshape(-1, SUBLANES, LANES)


def kernel(x_prompt, x_sample, c_prompt, c_sample, cache_win_k, cache_win_v, state_conv, w_ada, b_ada, norm1, w_in, q_norm, k_norm, attn_sinks, conv_w, w_attn_o, w_conv_o, w_out, norm2, w_peer_q, peer_sub_keys, peer_u, peer_v):
    depth = w_ada.shape[0]
    bp, sp, _ = x_prompt.shape
    bs, ss, _ = x_sample.shape
    assert sp % ATTN_BLOCK == 0 and ss <= SUBLANES and (bs * ss) % (2 * PEER_TOK) == 0
    tb_p = 256 if sp % 256 == 0 else ATTN_BLOCK
    pos_p = jnp.arange(sp)
    pos_s = PAST_LEN + jnp.arange(ss)
    yp = x_prompt.reshape(bp * sp, D_MODEL)
    ys = x_sample.transpose(1, 0, 2).reshape(ss * bs, D_MODEL)
    kp_l, vp_l, cp_l, ks_l, vs_l, cs_l = [], [], [], [], [], []
    for l in range(depth):
        lw = {
            "norm1": norm1[l].reshape(1, D_MODEL), "norm2": norm2[l].reshape(1, D_MODEL),
            "w_in": w_in[l].astype(BF16),
            "qn": jnp.tile(q_norm[l], N_HEADS).reshape(1, Q_W), "kn": jnp.tile(k_norm[l], N_KV_HEADS).reshape(1, KV_W),
            "sinks": attn_sinks[l], "conv_w": conv_w[l],
            "wa": w_attn_o[l].astype(BF16), "wc": w_conv_o[l].astype(BF16), "wo": w_out[l].astype(BF16),
            "wpq": w_peer_q[l].astype(BF16),
            "sk": peer_sub_keys[l].reshape(2 * PEER_HEADS, N_KEYS, PEER_HALF).astype(BF16),
            "uv": _pack_uv(peer_u[l], peer_v[l]),
        }
        c_all = jnp.concatenate([c_prompt, c_sample], axis=0)
        pad = (-c_all.shape[0]) % SUBLANES
        mod = _mod(jnp.pad(c_all, ((0, pad), (0, 0))), w_ada[l], b_ada[l])
        mods_p = tuple(m.reshape(bp, 1, D_MODEL) for m in jnp.split(mod[:bp], 6, axis=-1))
        mods_s = tuple(jnp.split(mod[bp:bp + bs], 6, axis=-1))
        x1, h2, idx, g_tok, ksn, vsn, us = _dense(True, ys, mods_s, pos_s, lw, tb=bs, steps_per_batch=1, batch=bs,
                                                  seq=ss, win_k=cache_win_k[l], win_v=cache_win_v[l],
                                                  conv_state=state_conv[l].transpose(1, 0, 2))
        ys = _final(True, x1, mods_s[5], _tc_peer(lw, idx, h2, g_tok), tb=bs, steps_per_batch=1)
        seqs = [_dense(False, yp[b * sp:(b + 1) * sp], tuple(m[b:b + 1] for m in mods_p), pos_p, lw, tb=tb_p,
                       steps_per_batch=sp // tb_p, batch=1, seq=sp) for b in range(bp)]
        out = _prompt_peer(lw, [(s[2], s[1], s[3]) for s in seqs])
        yp = _final(False, jnp.concatenate([s[0] for s in seqs], axis=0), mods_p[5], out, tb=tb_p,
                    steps_per_batch=sp // tb_p)
        kp_l.append(jnp.stack([s[4][-WINDOW:].reshape(WINDOW, N_KV_HEADS, HEAD_DIM) for s in seqs]))
        vp_l.append(jnp.stack([s[5][-WINDOW:].reshape(WINDOW, N_KV_HEADS, HEAD_DIM) for s in seqs]))
        cp_l.append(jnp.stack([s[6][-(CONV_WIDTH - 1):] for s in seqs]))
        k_new = ksn.reshape(ss, bs, N_KV_HEADS, HEAD_DIM).transpose(1, 0, 2, 3)
        v_new = vsn.reshape(ss, bs, N_KV_HEADS, HEAD_DIM).transpose(1, 0, 2, 3)
        ks_l.append(jnp.concatenate([cache_win_k[l], k_new], axis=1)[:, -WINDOW:])
        vs_l.append(jnp.concatenate([cache_win_v[l], v_new], axis=1)[:, -WINDOW:])
        ext = jnp.concatenate([state_conv[l], us.reshape(ss, bs, D_CONV).transpose(1, 0, 2)], axis=1)
        cs_l.append(ext[:, -(CONV_WIDTH - 1):])
    return (yp.reshape(bp, sp, D_MODEL), ys.reshape(ss, bs, D_MODEL).transpose(1, 0, 2),
            jnp.stack(kp_l), jnp.stack(vp_l), jnp.stack(cp_l), jnp.stack(ks_l), jnp.stack(vs_l), jnp.stack(cs_l))
```

```python
import functools

import numpy as np
import jax
import jax.numpy as jnp
from jax import lax
from jax.experimental import pallas as pl
from jax.experimental.pallas import tpu as pltpu
from jax.experimental.pallas import tpu_sc as plsc

F32 = jnp.float32
BF16 = jnp.bfloat16
HIGHEST = lax.Precision.HIGHEST

D_MODEL = 1024
N_HEADS = 16
HEAD_DIM = 64
N_KV_HEADS = 4
GROUP = N_HEADS // N_KV_HEADS
WINDOW = 128
ATTN_BLOCK = 128
ROPE_THETA = 10000.0
Q_W = N_HEADS * HEAD_DIM
KV_W = N_KV_HEADS * HEAD_DIM
D_CONV = 1024
CONV_WIDTH = 3
PEER_HEADS = 8
N_KEYS = 128
PEER_TOPK = 16
PEER_HALF = 128
PEER_SEL = PEER_HEADS * PEER_TOPK
PAST_LEN = 16384
EPS = 1e-6
NEG = -1e30
IN_SIZES = (Q_W, KV_W, KV_W, D_CONV, D_CONV, D_CONV, D_MODEL, D_MODEL)
IN_OFFS = tuple(int(v) for v in np.cumsum((0,) + IN_SIZES))
D_IN = IN_OFFS[-1]

LANES = 128
SUBLANES = 8
VMEM_LIMIT = 56 * 1024 * 1024
PEER_TOK = 8
POST_TILES = 2
PEER_DMA_QUEUES = 2
SC_CORES = 2
SC_SUBCORES = 16
SC_LANES = 16
SC_WORKERS = SC_CORES * SC_SUBCORES
SC_PASS = 32
SC_TOKEN_SHARE = 0.60
PEER_PAIRS = tuple((i, j) for i in range(PEER_TOPK) for j in range(PEER_TOPK) if (i + 1) * (j + 1) <= PEER_TOPK)


def _const_spec(shape):
    nd = len(shape)
    return pl.BlockSpec(shape, lambda *_: (0,) * nd, pipeline_mode=pl.Buffered(1))


def _params(n_grid):
    return pltpu.CompilerParams(dimension_semantics=("arbitrary",) * n_grid, vmem_limit_bytes=VMEM_LIMIT)


def _mod_kernel(c_ref, w_ref, b_ref, o_ref):
    c = c_ref[...]
    s = c * jax.nn.sigmoid(c)
    o_ref[...] = jnp.dot(s, w_ref[...], precision=HIGHEST, preferred_element_type=F32) + b_ref[...]


def _mod(c, w_ada, b_ada):
    rows, n = c.shape[0], w_ada.shape[1]
    tn = 512
    return pl.pallas_call(
        _mod_kernel,
        grid=(n // tn,),
        in_specs=[pl.BlockSpec((rows, D_MODEL), lambda j: (0, 0)),
                  pl.BlockSpec((D_MODEL, tn), lambda j: (0, j)),
                  pl.BlockSpec((1, tn), lambda j: (0, j))],
        out_specs=pl.BlockSpec((rows, tn), lambda j: (0, j)),
        out_shape=jax.ShapeDtypeStruct((rows, n), F32),
        compiler_params=_params(1),
        name="mod",
    )(c, w_ada, b_ada.reshape(1, n))


def _dot_hi_lo(x, m01):
    hi = x.astype(BF16)
    lo = (x - hi.astype(F32)).astype(BF16)
    m = m01.astype(BF16)
    return jnp.dot(hi, m, preferred_element_type=F32) + jnp.dot(lo, m, preferred_element_type=F32)


def _head_norm(z, g_ref, gt_ref, w):
    ss = _dot_hi_lo(z * z, g_ref[...])
    ssb = _dot_hi_lo(ss, gt_ref[...])
    return z * lax.rsqrt(ssb * (1.0 / HEAD_DIM) + EPS) * w


def _rope(x, cos, sin_signed):
    w = x.shape[1]
    lane = lax.broadcasted_iota(jnp.int32, x.shape, 1)
    fwd = pltpu.roll(x, w - HEAD_DIM // 2, axis=1)
    bwd = pltpu.roll(x, HEAD_DIM // 2, axis=1)
    partner = jnp.where((lane % HEAD_DIM) < HEAD_DIM // 2, fwd, bwd)
    reps = w // LANES
    return x * jnp.tile(cos, (1, reps)) + partner * jnp.tile(sin_signed, (1, reps))


def _in_kernel(sample, tb, steps_per_batch, *refs):
    if sample:
        (x_ref, sh_ref, sc_ref, n1_ref, w_ref, qn_ref, kn_ref, cos_ref, sin_ref, cw_ref,
         gq_ref, gqt_ref, gk_ref, gkt_ref, st_ref,
         q_ref, k_ref, v_ref, bc_ref, ga_ref, gc_ref, u_ref, prev_ref) = refs
    else:
        (x_ref, sh_ref, sc_ref, n1_ref, w_ref, qn_ref, kn_ref, cos_ref, sin_ref, cw_ref,
         gq_ref, gqt_ref, gk_ref, gkt_ref,
         q_ref, k_ref, v_ref, bc_ref, ga_ref, gc_ref, u_ref, ext_ref) = refs
    i = pl.program_id(0)
    x = x_ref[...]
    ms = jnp.mean(x * x, axis=-1, keepdims=True)
    h = (x * lax.rsqrt(ms + EPS) * n1_ref[...]) * (1.0 + sc_ref[...]) + sh_ref[...]
    hb = h.astype(BF16)

    def seg(n):
        return jnp.dot(hb, w_ref[:, IN_OFFS[n]:IN_OFFS[n + 1]], preferred_element_type=F32)

    cos, sin = cos_ref[...], sin_ref[...]
    q = _rope(_head_norm(seg(0), gq_ref, gqt_ref, qn_ref[...]), cos, sin)
    q_ref[...] = (q * (HEAD_DIM ** -0.5)).astype(BF16)
    k_ref[...] = _rope(_head_norm(seg(1), gk_ref, gkt_ref, kn_ref[...]), cos, sin)
    v_ref[...] = seg(2)
    u = seg(4) * seg(5)
    w0, w1, w2 = cw_ref[0:1, :], cw_ref[1:2, :], cw_ref[2:3, :]
    if sample:
        @pl.when(i == 0)
        def _():
            prev_ref[...] = st_ref[...]
        p2, p1 = prev_ref[0], prev_ref[1]
        conv = p2 * w0 + p1 * w1 + u * w2
        prev_ref[0] = p1
        prev_ref[1] = u
        u_ref[...] = u
    else:
        @pl.when(i % steps_per_batch == 0)
        def _():
            ext_ref[0:SUBLANES, :] = jnp.zeros((SUBLANES, D_CONV), F32)

        @pl.when(i % steps_per_batch != 0)
        def _():
            ext_ref[0:SUBLANES, :] = ext_ref[tb:tb + SUBLANES, :]
        ext_ref[SUBLANES:tb + SUBLANES, :] = u
        conv = ext_ref[SUBLANES - 2:tb + SUBLANES - 2, :] * w0 + ext_ref[SUBLANES - 1:tb + SUBLANES - 1, :] * w1 + u * w2
        u_ref[...] = u[tb - SUBLANES:tb, :]
    bc_ref[...] = (seg(3) * conv).astype(BF16)
    ga_ref[...] = jax.nn.sigmoid(seg(6))
    gc_ref[...] = jax.nn.sigmoid(seg(7))


def _group_mats(width):
    lane = np.arange(width)
    g = (lane[:, None] // HEAD_DIM == np.arange(LANES)[None, :]).astype(np.float32)
    return jnp.asarray(g), jnp.asarray(g.T)


def _in_proj(sample, x, sh, sc, norm1, w_in_bf, qn, kn, cos, sin, conv_w, state=None, *, tb, steps_per_batch):
    t = x.shape[0]
    steps = t // tb
    gq, gqt = _group_mats(Q_W)
    gk, gkt = _group_mats(KV_W)
    row = lambda w: pl.BlockSpec((tb, w), lambda i: (i, 0))
    if sample:
        mod_spec = pl.BlockSpec((tb, D_MODEL), lambda i: (0, 0))
        rope_spec = pl.BlockSpec((None, 1, LANES), lambda i: (i, 0, 0))
        u_spec, u_rows = row(D_CONV), t
        scratch = [pltpu.VMEM((2, tb, D_CONV), F32)]
    else:
        mod_spec = pl.BlockSpec((None, 1, D_MODEL), lambda i: (i // steps_per_batch, 0, 0))
        rope_spec = pl.BlockSpec((tb, LANES), lambda i: (i % steps_per_batch, 0))
        u_spec, u_rows = pl.BlockSpec((SUBLANES, D_CONV), lambda i: (i, 0)), steps * SUBLANES
        scratch = [pltpu.VMEM((tb + SUBLANES, D_CONV), F32)]
    in_specs = [row(D_MODEL), mod_spec, mod_spec, _const_spec((1, D_MODEL)), _const_spec((D_MODEL, D_IN)),
                _const_spec((1, Q_W)), _const_spec((1, KV_W)), rope_spec, rope_spec, _const_spec((CONV_WIDTH, D_CONV)),
                _const_spec(gq.shape), _const_spec(gqt.shape), _const_spec(gk.shape), _const_spec(gkt.shape)]
    args = [x, sh, sc, norm1, w_in_bf, qn, kn, cos, sin, conv_w, gq, gqt, gk, gkt]
    if sample:
        in_specs.append(_const_spec(state.shape))
        args.append(state)
    return pl.pallas_call(
        functools.partial(_in_kernel, sample, tb, steps_per_batch),
        grid=(steps,),
        in_specs=in_specs,
        out_specs=[row(Q_W), row(KV_W), row(KV_W), row(D_CONV), row(D_MODEL), row(D_MODEL), u_spec],
        out_shape=[jax.ShapeDtypeStruct((t, Q_W), BF16), jax.ShapeDtypeStruct((t, KV_W), F32),
                   jax.ShapeDtypeStruct((t, KV_W), F32), jax.ShapeDtypeStruct((t, D_CONV), BF16),
                   jax.ShapeDtypeStruct((t, D_MODEL), F32), jax.ShapeDtypeStruct((t, D_MODEL), F32),
                   jax.ShapeDtypeStruct((u_rows, D_CONV), F32)],
        scratch_shapes=scratch,
        compiler_params=_params(1),
        name="in_proj_sample" if sample else "in_proj_prompt",
    )(*args)


def _sink_softmax_pv(q, kg, vg, valid, sink):
    s = lax.dot_general(q, kg, (((1,), (1,)), ((), ())), preferred_element_type=F32)
    s = jnp.where(valid, s, NEG)
    m = jnp.maximum(jnp.max(s, axis=1, keepdims=True), sink)
    p = jnp.exp(s - m)
    denom = jnp.sum(p, axis=1, keepdims=True) + jnp.exp(sink - m)
    return jnp.dot(p.astype(BF16), vg, preferred_element_type=F32) / denom


def _attn_prompt_kernel(sink_ref, q_ref, kc_ref, kp_ref, vc_ref, vp_ref, o_ref):
    j = pl.program_id(1)
    q = q_ref[...]
    kb = jnp.concatenate([kp_ref[...], kc_ref[...]], axis=0).astype(BF16)
    vb = jnp.concatenate([vp_ref[...], vc_ref[...]], axis=0).astype(BF16)
    qi = lax.broadcasted_iota(jnp.int32, (ATTN_BLOCK, 2 * ATTN_BLOCK), 0)
    kj = lax.broadcasted_iota(jnp.int32, (ATTN_BLOCK, 2 * ATTN_BLOCK), 1)
    valid = (kj >= qi) & (kj <= qi + WINDOW) & ((kj >= ATTN_BLOCK) | (j > 0))
    outs = []
    for h in range(N_HEADS):
        g = h // GROUP
        kg = kb[:, g * HEAD_DIM:(g + 1) * HEAD_DIM]
        vg = vb[:, g * HEAD_DIM:(g + 1) * HEAD_DIM]
        outs.append(_sink_softmax_pv(q[:, h * HEAD_DIM:(h + 1) * HEAD_DIM], kg, vg, valid, sink_ref[h]))
    o_ref[...] = jnp.concatenate(outs, axis=1).astype(BF16)


def _attn_prompt(q, k, v, sinks, batch, seq):
    nb = seq // ATTN_BLOCK
    cur = lambda w: pl.BlockSpec((ATTN_BLOCK, w), lambda b, j: (b * nb + j, 0))
    prev = lambda w: pl.BlockSpec((ATTN_BLOCK, w), lambda b, j: (b * nb + jnp.maximum(j - 1, 0), 0))
    return pl.pallas_call(
        _attn_prompt_kernel,
        grid=(batch, nb),
        in_specs=[pl.BlockSpec(memory_space=pltpu.SMEM), cur(Q_W), cur(KV_W), prev(KV_W), cur(KV_W), prev(KV_W)],
        out_specs=cur(Q_W),
        out_shape=jax.ShapeDtypeStruct((batch * seq, Q_W), BF16),
        compiler_params=_params(2),
        name="attn_prompt",
    )(sinks, q, k, k, v, v)


def _attn_sample_kernel(bb, n_new, sink_ref, q_ref, kn_ref, vn_ref, kc_ref, vc_ref, o_ref):
    rows = GROUP * n_new
    s_len = 2 * WINDOW
    ri = lax.broadcasted_iota(jnp.int32, (rows, s_len), 0)
    kj = lax.broadcasted_iota(jnp.int32, (rows, s_len), 1)
    t = ri % n_new
    valid = ((kj < WINDOW) & (kj >= t)) | ((kj >= WINDOW) & (kj - WINDOW <= t))
    pad = jnp.zeros((WINDOW - SUBLANES, KV_W), F32)
    for b in range(bb):
        kb = jnp.concatenate([kc_ref[b], kn_ref[b], pad], axis=0).astype(BF16)
        vb = jnp.concatenate([vc_ref[b], vn_ref[b], pad], axis=0).astype(BF16)
        for g in range(N_KV_HEADS):
            kg = kb[:, g * HEAD_DIM:(g + 1) * HEAD_DIM]
            vg = vb[:, g * HEAD_DIM:(g + 1) * HEAD_DIM]
            o_ref[b, g] = _sink_softmax_pv(q_ref[b, g], kg, vg, valid, sink_ref[:, g:g + 1]).astype(BF16)


def _attn_sample(q4, k_new, v_new, cache_k, cache_v, sink_mat, n_new):
    nbatch = q4.shape[0]
    bb = 8 if nbatch % 8 == 0 else nbatch
    rows = GROUP * n_new
    b3 = lambda r, w: pl.BlockSpec((bb, r, w), lambda i: (i, 0, 0))
    qspec = pl.BlockSpec((bb, N_KV_HEADS, rows, HEAD_DIM), lambda i: (i, 0, 0, 0))
    return pl.pallas_call(
        functools.partial(_attn_sample_kernel, bb, n_new),
        grid=(nbatch // bb,),
        in_specs=[_const_spec(sink_mat.shape), qspec, b3(SUBLANES, KV_W), b3(SUBLANES, KV_W),
                  b3(WINDOW, KV_W), b3(WINDOW, KV_W)],
        out_specs=qspec,
        out_shape=jax.ShapeDtypeStruct(q4.shape, BF16),
        compiler_params=_params(1),
        name="attn_sample",
    )(sink_mat, q4, k_new, v_new, cache_k, cache_v)


def _topk_rows(s, k, payload=None):
    r = s.shape[0]
    rows = lax.broadcasted_iota(jnp.int32, s.shape, 0)
    vals, sel = [], []
    for _ in range(k):
        m = jnp.max(s, axis=0, keepdims=True)
        ix = jnp.min(jnp.where(s == m, rows, r), axis=0, keepdims=True)
        hit = rows == ix
        vals.append(m)
        sel.append(ix if payload is None else jnp.max(jnp.where(hit, payload, -1), axis=0, keepdims=True))
        s = jnp.where(hit, -jnp.inf, s)
    return jnp.concatenate(vals, axis=0), jnp.concatenate(sel, axis=0)


def _peer_route(q0, q1, k0, k1):
    nt = (((1,), (1,)), ((), ()))
    s0 = lax.dot_general(k0, q0, nt, preferred_element_type=F32)
    s1 = lax.dot_general(k1, q1, nt, preferred_element_type=F32)
    sv0, si0 = _topk_rows(s0, PEER_TOPK)
    sv1, si1 = _topk_rows(s1, PEER_TOPK)
    tb = s0.shape[1]
    row8 = lax.broadcasted_iota(jnp.int32, (SUBLANES, tb), 0)
    n_grp = -(-len(PEER_PAIRS) // SUBLANES)
    cvs, ces = [], []
    for gi in range(n_grp):
        cv = jnp.full((SUBLANES, tb), -jnp.inf, F32)
        ce = jnp.zeros((SUBLANES, tb), jnp.int32)
        for r in range(SUBLANES):
            c = gi * SUBLANES + r
            if c < len(PEER_PAIRS):
                i, j = PEER_PAIRS[c]
                cv = jnp.where(row8 == r, sv0[i:i + 1, :] + sv1[j:j + 1, :], cv)
                ce = jnp.where(row8 == r, si0[i:i + 1, :] * N_KEYS + si1[j:j + 1, :], ce)
        cvs.append(cv)
        ces.append(ce)
    cs, e = _topk_rows(jnp.concatenate(cvs, axis=0), PEER_TOPK, payload=jnp.concatenate(ces, axis=0))
    p = jnp.exp(cs - cs[0:1, :])
    return e, p / jnp.sum(p, axis=0, keepdims=True)


def _post_kernel(tb, x_ref, at_ref, bc_ref, ga_ref, gc_ref, gt1_ref, sh2_ref, sc2_ref, n2_ref,
                 wa_ref, wc_ref, wo_ref, wpq_ref, sk_ref, x1_ref, h2_ref, e_ref, g_ref, qp_ref):
    merged = (ga_ref[...] * jnp.dot(at_ref[...], wa_ref[...], preferred_element_type=F32)
              + gc_ref[...] * jnp.dot(bc_ref[...], wc_ref[...], preferred_element_type=F32))
    x1 = x_ref[...] + gt1_ref[...] * jnp.dot(merged.astype(BF16), wo_ref[...], preferred_element_type=F32)
    x1_ref[...] = x1
    ms = jnp.mean(x1 * x1, axis=-1, keepdims=True)
    h2 = (x1 * lax.rsqrt(ms + EPS) * n2_ref[...]) * (1.0 + sc2_ref[...]) + sh2_ref[...]
    h2_ref[...] = h2
    qp = jnp.dot(h2.astype(BF16), wpq_ref[...], preferred_element_type=F32).astype(BF16)
    for hp in range(2 * PEER_HEADS):
        qp_ref[hp] = qp[:, hp * PEER_HALF:(hp + 1) * PEER_HALF]

    def head(h, carry):
        e, g = _peer_route(qp_ref[2 * h], qp_ref[2 * h + 1], sk_ref[2 * h], sk_ref[2 * h + 1])
        e_ref[h] = e
        g_ref[h] = g
        return carry

    lax.fori_loop(0, PEER_HEADS, head, 0)


def _post(sample, x, attn, bc, ga, gc, gt1, sh2, sc2, norm2, wa, wc, wo, wpq, sk, *, tb, steps_per_batch):
    t = x.shape[0]
    row = lambda w: pl.BlockSpec((tb, w), lambda i: (i, 0))
    if sample:
        mod_spec = pl.BlockSpec((tb, D_MODEL), lambda i: (0, 0))
    else:
        mod_spec = pl.BlockSpec((None, 1, D_MODEL), lambda i: (i // steps_per_batch, 0, 0))
    sel_spec = pl.BlockSpec((PEER_HEADS, PEER_TOPK, tb), lambda i: (0, 0, i))
    return pl.pallas_call(
        functools.partial(_post_kernel, tb),
        grid=(t // tb,),
        in_specs=[row(D_MODEL), row(Q_W), row(D_CONV), row(D_MODEL), row(D_MODEL), mod_spec, mod_spec, mod_spec,
                  _const_spec((1, D_MODEL)), _const_spec(wa.shape), _const_spec(wc.shape), _const_spec(wo.shape),
                  _const_spec(wpq.shape), _const_spec(sk.shape)],
        out_specs=[row(D_MODEL), row(D_MODEL), sel_spec, sel_spec],
        out_shape=[jax.ShapeDtypeStruct((t, D_MODEL), F32), jax.ShapeDtypeStruct((t, D_MODEL), F32),
                   jax.ShapeDtypeStruct((PEER_HEADS, PEER_TOPK, t), jnp.int32),
                   jax.ShapeDtypeStruct((PEER_HEADS, PEER_TOPK, t), F32)],
        scratch_shapes=[pltpu.VMEM((2 * PEER_HEADS, tb, PEER_HALF), BF16)],
        compiler_params=_params(1),
        name="post_sample" if sample else "post_prompt",
    )(x, attn, bc, ga, gc, gt1, sh2, sc2, norm2, wa, wc, wo, wpq, sk)


def _sublane_sums(p):
    sub = lax.broadcasted_iota(jnp.int32, (SUBLANES, LANES), 0)

    def merge(a, b, sh):
        lo = (sub & sh) == 0
        fa = a + pltpu.roll(a, SUBLANES - sh, axis=0)
        fb = b + pltpu.roll(b, sh, axis=0)
        return jnp.where(lo, fa, fb)

    n = [merge(p[i], p[i + 4], 4) for i in range(4)]
    q = [merge(n[0], n[2], 2), merge(n[1], n[3], 2)]
    return merge(q[0], q[1], 1)


def _u_half(word):
    return lax.bitcast_convert_type(word & jnp.int32(-65536), F32)


def _v_half(word):
    return lax.bitcast_convert_type(word << 16, F32)


def _peer_kernel(n, idx_cur, idx_nxt, h_ref, g_ref, uv_hbm, o_ref, buf0, buf1, wbuf, sem):
    i = pl.program_id(0)
    phase_rows = PEER_TOK * PEER_SEL
    bufs = (buf0, buf1)

    def token_rows(s, t):
        return bufs[s].at[pl.ds(t * PEER_SEL, PEER_SEL)]

    def issue_token(idx_ref, off, s, t):
        for k in range(PEER_SEL):
            r = t * PEER_SEL + k
            pltpu.make_async_copy(uv_hbm.at[idx_ref[off + r]], bufs[s].at[r], sem.at[s, t]).start(
                priority=k % PEER_DMA_QUEUES)

    def wait_token(s, t):
        pltpu.make_async_copy(uv_hbm.at[pl.ds(0, PEER_SEL)], token_rows(s, t), sem.at[s, t]).wait()

    @pl.when(i == 0)
    def _():
        def body(r, c):
            pltpu.make_async_copy(uv_hbm.at[idx_cur[r]], buf0.at[r], sem.at[0, r // PEER_SEL]).start()
            return c
        lax.fori_loop(0, phase_rows, body, 0)

    eye = (lax.broadcasted_iota(jnp.int32, (PEER_SEL, PEER_SEL), 0)
           == lax.broadcasted_iota(jnp.int32, (PEER_SEL, PEER_SEL), 1)).astype(F32)
    gcol = lax.dot_general(eye, g_ref[...], (((1,), (1,)), ((), ())), precision=HIGHEST, preferred_element_type=F32)
    for s in range(2):
        for t in range(PEER_TOK):
            wait_token(s, t)
            if s == 0:
                issue_token(idx_cur, phase_rows, 1, t)
            else:
                issue_token(idx_nxt, 0, 0, t)
            tok = s * PEER_TOK + t
            h = h_ref[tok]
            base = t * PEER_SEL
            parts = []
            for k0 in range(0, PEER_SEL, SUBLANES):
                parts.append(_sublane_sums([_u_half(bufs[s][base + k0 + j]) * h for j in range(SUBLANES)]))
            a = jnp.sum(jnp.concatenate(parts, axis=0), axis=1, keepdims=True)
            w = gcol[:, tok:tok + 1] * jax.nn.gelu(a)
            wbuf[...] = jnp.broadcast_to(w, (PEER_SEL, LANES))
            accs = [jnp.zeros((SUBLANES, LANES), F32) for _ in range(4)]
            for k in range(PEER_SEL):
                accs[k % 4] = accs[k % 4] + wbuf[k:k + 1, :] * _v_half(bufs[s][base + k])
            o_ref[tok] = (accs[0] + accs[1]) + (accs[2] + accs[3])

    @pl.when(i == n - 1)
    def _():
        for t in range(PEER_TOK):
            wait_token(0, t)


def _peer(idx_flat, h3, g_tok, uv3):
    t = h3.shape[0]
    step_tok = 2 * PEER_TOK
    n = t // step_tok
    rows = step_tok * PEER_SEL
    tile = pl.BlockSpec((step_tok, SUBLANES, LANES), lambda i: (i, 0, 0))
    return pl.pallas_call(
        functools.partial(_peer_kernel, n),
        grid=(n,),
        in_specs=[pl.BlockSpec((rows,), lambda i: (i,), memory_space=pltpu.SMEM),
                  pl.BlockSpec((rows,), lambda i: (jnp.minimum(i + 1, n - 1),), memory_space=pltpu.SMEM),
                  tile,
                  pl.BlockSpec((step_tok, PEER_SEL), lambda i: (i, 0)),
                  pl.BlockSpec(memory_space=pl.ANY)],
        out_specs=tile,
        out_shape=jax.ShapeDtypeStruct((t, SUBLANES, LANES), F32),
        scratch_shapes=[pltpu.VMEM((PEER_TOK * PEER_SEL, SUBLANES, LANES), jnp.int32),
                        pltpu.VMEM((PEER_TOK * PEER_SEL, SUBLANES, LANES), jnp.int32),
                        pltpu.VMEM((PEER_SEL, LANES), F32),
                        pltpu.SemaphoreType.DMA((2, PEER_TOK))],
        compiler_params=pltpu.CompilerParams(dimension_semantics=("arbitrary",), vmem_limit_bytes=VMEM_LIMIT,
                                             disable_bounds_checks=True),
        name="peer_mix",
    )(idx_flat, idx_flat, h3, g_tok, uv3)


def _gelu_tanh_via_exp(a):
    z = 0.7978845608028654 * (a + 0.044715 * (a * a * a))
    t = 1.0 - 2.0 / (jnp.exp(2.0 * z) + 1.0)
    return 0.5 * a * (1.0 + t)


def _sc_peer(tab, idx, h, g):
    t_all = idx.shape[0]
    per_w = t_all // SC_WORKERS
    n_pass = PEER_SEL // SC_PASS
    n_chunk = D_MODEL // SC_LANES
    quarter = 16 * SC_LANES
    mesh = plsc.VectorSubcoreMesh(core_axis_name="c", subcore_axis_name="s")

    @functools.partial(
        pl.kernel, mesh=mesh,
        out_type=jax.ShapeDtypeStruct((t_all, D_MODEL), F32),
        scratch_types=[pltpu.VMEM((2, PEER_SEL), jnp.int32),
                       pltpu.VMEM((2, SC_PASS, SUBLANES, LANES), jnp.int32),
                       pltpu.VMEM((2, D_MODEL), F32),
                       pltpu.VMEM((2, PEER_SEL), F32),
                       pltpu.VMEM((SC_PASS, SC_LANES), F32),
                       pltpu.VMEM((PEER_SEL,), F32),
                       pltpu.VMEM((D_MODEL,), F32),
                       pltpu.SemaphoreType.DMA((2,)),
                       pltpu.SemaphoreType.DMA((2,))],
        compiler_params=pltpu.CompilerParams(needs_layout_passes=False),
    )
    def k(tab_hbm, idx_hbm, h_hbm, g_hbm, out_hbm, idx_v, rows_v, h_v, g_v, ap_v, w_v, o_v, tsem, rsem):
        wid = lax.axis_index("s") * SC_CORES + lax.axis_index("c")
        lane = lax.iota(jnp.int32, SC_LANES)
        zero = jnp.zeros((SC_LANES,), F32)
        tok0 = wid * per_w

        def token_inputs(tok, slot):
            return (pltpu.make_async_copy(idx_hbm.at[tok], idx_v.at[slot], tsem.at[slot]),
                    pltpu.make_async_copy(h_hbm.at[tok], h_v.at[slot], tsem.at[slot]),
                    pltpu.make_async_copy(g_hbm.at[tok], g_v.at[slot], tsem.at[slot]))

        def gather(slot, q, b):
            ids = idx_v.at[slot].at[pl.ds(q * SC_PASS, SC_PASS)]
            return pltpu.make_async_copy(tab_hbm.at[ids], rows_v.at[b], rsem.at[b])

        def row_chunk(b, kk, jc, j):
            col = jc * quarter + j * SC_LANES
            return rows_v[b, kk, col // LANES, pl.ds(col % LANES, SC_LANES)]

        for c in token_inputs(tok0, 0):
            c.start()
        for c in token_inputs(tok0, 0):
            c.wait()
        gather(0, 0, 0).start()

        def token(ti, carry):
            ts = lax.rem(ti, 2)
            tn = 1 - ts
            tok = tok0 + ti
            nxt = tok0 + jnp.minimum(ti + 1, per_w - 1)
            for c in token_inputs(nxt, tn):
                c.start()
            for j in range(n_chunk):
                o_v[pl.ds(j * SC_LANES, SC_LANES)] = zero
            for q in range(n_pass):
                b = q % 2
                gather(ts, q, b).wait()
                if q + 1 < n_pass:
                    gather(ts, q + 1, 1 - b).start()
                else:
                    for c in token_inputs(nxt, tn):
                        c.wait()
                    gather(tn, 0, 1 - b).start()
                for jc in range(D_MODEL // quarter):
                    hregs = [h_v[ts, pl.ds(jc * quarter + j * SC_LANES, SC_LANES)] for j in range(16)]

                    def dot_rows(kk, jc=jc, hregs=hregs, b=b):
                        part = [None] * 4
                        for j in range(16):
                            p = _u_half(row_chunk(b, kk, jc, j)) * hregs[j]
                            part[j % 4] = p if part[j % 4] is None else part[j % 4] + p
                        acc = (part[0] + part[1]) + (part[2] + part[3])
                        if jc == 0:
                            ap_v[kk, :] = acc
                        else:
                            plsc.addupdate(ap_v.at[kk], acc)

                    plsc.parallel_loop(0, SC_PASS)(dot_rows)
                for m in range(SC_PASS // SC_LANES):
                    rowi = lane + m * SC_LANES
                    a = zero
                    for col in range(SC_LANES):
                        a = a + plsc.load_gather(ap_v, [rowi, jnp.full((SC_LANES,), col, jnp.int32)])
                    off = q * SC_PASS + m * SC_LANES
                    w_v[pl.ds(off, SC_LANES)] = g_v[ts, pl.ds(off, SC_LANES)] * _gelu_tanh_via_exp(a)
                for jc in range(D_MODEL // quarter):
                    def mix_rows(kk, accs, q=q, jc=jc, b=b):
                        ws = plsc.load_gather(w_v, [jnp.full((SC_LANES,), q * SC_PASS, jnp.int32) + kk])
                        return tuple(accs[j] + ws * _v_half(row_chunk(b, kk, jc, j)) for j in range(16))

                    accs = plsc.parallel_loop(0, SC_PASS, carry=tuple(zero for _ in range(16)))(mix_rows)
                    for j in range(16):
                        plsc.addupdate(o_v.at[pl.ds(jc * quarter + j * SC_LANES, SC_LANES)], accs[j])
            pltpu.sync_copy(o_v, out_hbm.at[tok])
            return carry

        lax.fori_loop(0, per_w, token, 0)
        gather(0, 0, 0).wait()

    return k(tab, idx, h, g)


def _final_kernel(x_ref, gt_ref, o_ref, y_ref):
    y_ref[...] = x_ref[...] + gt_ref[...] * o_ref[...]


def _final(sample, x1, gt2, out, *, tb, steps_per_batch):
    t = x1.shape[0]
    row = pl.BlockSpec((tb, D_MODEL), lambda i: (i, 0))
    if sample:
        mod_spec = pl.BlockSpec((tb, D_MODEL), lambda i: (0, 0))
    else:
        mod_spec = pl.BlockSpec((None, 1, D_MODEL), lambda i: (i // steps_per_batch, 0, 0))
    return pl.pallas_call(
        _final_kernel,
        grid=(t // tb,),
        in_specs=[row, mod_spec, row],
        out_specs=row,
        out_shape=jax.ShapeDtypeStruct((t, D_MODEL), F32),
        compiler_params=_params(1),
        name="final_sample" if sample else "final_prompt",
    )(x1, gt2, out)


def _rope_tables(pos):
    half = HEAD_DIM // 2
    inv = jnp.power(jnp.float32(ROPE_THETA), -jnp.arange(half, dtype=F32) / half)
    ang = pos.astype(F32)[:, None] * inv[None, :]
    lane = np.arange(LANES)
    sign = jnp.asarray(np.where((lane % HEAD_DIM) < half, -1.0, 1.0).astype(np.float32))
    cos = jnp.cos(ang)[:, lane % half]
    sin = jnp.sin(ang)[:, lane % half] * sign[None, :]
    return cos, sin


def _dense(sample, x, mods, pos, lw, *, tb, steps_per_batch, batch, seq, win_k=None, win_v=None, conv_state=None):
    sh1, sc1, gt1, sh2, sc2, _ = mods
    cos, sin = _rope_tables(pos)
    if sample:
        cos, sin = cos[:, None, :], sin[:, None, :]
    q, k, v, bc, ga, gc, u_out = _in_proj(sample, x, sh1, sc1, lw["norm1"], lw["w_in"], lw["qn"], lw["kn"], cos, sin,
                                          lw["conv_w"], conv_state, tb=tb, steps_per_batch=steps_per_batch)
    if sample:
        n_new, nb = seq, batch
        q4 = q.reshape(n_new, nb, N_KV_HEADS, GROUP, HEAD_DIM).transpose(1, 2, 3, 0, 4).reshape(
            nb, N_KV_HEADS, GROUP * n_new, HEAD_DIM)
        to_b = lambda a: jnp.pad(a.reshape(n_new, nb, KV_W).transpose(1, 0, 2), ((0, 0), (0, SUBLANES - n_new), (0, 0)))
        sink_mat = jnp.repeat(lw["sinks"].reshape(N_KV_HEADS, GROUP), n_new, axis=1).T
        o4 = _attn_sample(q4, to_b(k), to_b(v), win_k.reshape(nb, WINDOW, KV_W), win_v.reshape(nb, WINDOW, KV_W),
                          sink_mat, n_new)
        attn = o4.reshape(nb, N_KV_HEADS, GROUP, n_new, HEAD_DIM).transpose(3, 0, 1, 2, 4).reshape(n_new * nb, Q_W)
    else:
        attn = _attn_prompt(q, k, v, lw["sinks"], batch, seq)
    pt = 1 if sample or steps_per_batch % POST_TILES else POST_TILES
    x1, h2, e, g = _post(sample, x, attn, bc, ga, gc, gt1, sh2, sc2, lw["norm2"], lw["wa"], lw["wc"], lw["wo"],
                         lw["wpq"], lw["sk"], tb=tb * pt, steps_per_batch=steps_per_batch // pt)
    t = x.shape[0]
    return x1, h2, e.reshape(PEER_SEL, t).T, g.reshape(PEER_SEL, t).T, k, v, u_out


def _tc_peer(lw, idx, h2, g_tok):
    t = idx.shape[0]
    return _peer(idx.reshape(t * PEER_SEL), h2.reshape(t, SUBLANES, LANES), g_tok,
                 lw["uv"]).reshape(t, D_MODEL)


def _prompt_peer(lw, parts):
    unit = 2 * SC_WORKERS
    total = sum(p[0].shape[0] for p in parts)
    left = int(total * SC_TOKEN_SHARE) // unit * unit
    n_sc = [0] * len(parts)
    for i in reversed(range(len(parts))):
        n_sc[i] = min(left, parts[i][0].shape[0])
        left -= n_sc[i]
    sc_out = [_sc_peer(lw["uv"], idx[:n], h2[:n], g_tok[:n]) if n else None
              for (idx, h2, g_tok), n in zip(parts, n_sc)]
    rest = [(idx[n:], h2[n:], g_tok[n:]) for (idx, h2, g_tok), n in zip(parts, n_sc) if n < idx.shape[0]]
    tc_out = _tc_peer(lw, *(jnp.concatenate(c, axis=0) for c in zip(*rest))) if rest else None
    outs, off = [], 0
    for (idx, _, _), n, so in zip(parts, n_sc, sc_out):
        if so is not None:
            outs.append(so)
        m = idx.shape[0] - n
        if m:
            outs.append(tc_out[off:off + m])
            off += m
    return jnp.concatenate(outs, axis=0)


def _pack_uv(u, v):
    ub = lax.bitcast_convert_type(u.astype(BF16), jnp.uint16).astype(jnp.uint32)
    vb = lax.bitcast_convert_type(v.astype(BF16), jnp.uint16).astype(jnp.uint32)
    return lax.bitcast_convert_type((ub << 16) | vb, jnp.int32).reshape(-1, SUBLANES, LANES)


def kernel(x_prompt, x_sample, c_prompt, c_sample, cache_win_k, cache_win_v, state_conv, w_ada, b_ada, norm1, w_in, q_norm, k_norm, attn_sinks, conv_w, w_attn_o, w_conv_o, w_out, norm2, w_peer_q, peer_sub_keys, peer_u, peer_v):
    depth = w_ada.shape[0]
    bp, sp, _ = x_prompt.shape
    bs, ss, _ = x_sample.shape
    assert sp % ATTN_BLOCK == 0 and ss <= SUBLANES and (bs * ss) % (2 * PEER_TOK) == 0
    tb_p = 256 if sp % 256 == 0 else ATTN_BLOCK
    pos_p = jnp.arange(sp)
    pos_s = PAST_LEN + jnp.arange(ss)
    yp = x_prompt.reshape(bp * sp, D_MODEL)
    ys = x_sample.transpose(1, 0, 2).reshape(ss * bs, D_MODEL)
    kp_l, vp_l, cp_l, ks_l, vs_l, cs_l = [], [], [], [], [], []
    for l in range(depth):
        lw = {
            "norm1": norm1[l].reshape(1, D_MODEL), "norm2": norm2[l].reshape(1, D_MODEL),
            "w_in": w_in[l].astype(BF16),
            "qn": jnp.tile(q_norm[l], N_HEADS).reshape(1, Q_W), "kn": jnp.tile(k_norm[l], N_KV_HEADS).reshape(1, KV_W),
            "sinks": attn_sinks[l], "conv_w": conv_w[l],
            "wa": w_attn_o[l].astype(BF16), "wc": w_conv_o[l].astype(BF16), "wo": w_out[l].astype(BF16),
            "wpq": w_peer_q[l].astype(BF16),
            "sk": peer_sub_keys[l].reshape(2 * PEER_HEADS, N_KEYS, PEER_HALF).astype(BF16),
            "uv": _pack_uv(peer_u[l], peer_v[l]),
        }
        c_all = jnp.concatenate([c_prompt, c_sample], axis=0)
        pad = (-c_all.shape[0]) % SUBLANES
        mod = _mod(jnp.pad(c_all, ((0, pad), (0, 0))), w_ada[l], b_ada[l])
        mods_p = tuple(m.reshape(bp, 1, D_MODEL) for m in jnp.split(mod[:bp], 6, axis=-1))
        mods_s = tuple(jnp.split(mod[bp:bp + bs], 6, axis=-1))
        x1, h2, idx, g_tok, ksn, vsn, us = _dense(True, ys, mods_s, pos_s, lw, tb=bs, steps_per_batch=1, batch=bs,
                                                  seq=ss, win_k=cache_win_k[l], win_v=cache_win_v[l],
                                                  conv_state=state_conv[l].transpose(1, 0, 2))
        ys = _final(True, x1, mods_s[5], _tc_peer(lw, idx, h2, g_tok), tb=bs, steps_per_batch=1)
        seqs = [_dense(False, yp[b * sp:(b + 1) * sp], tuple(m[b:b + 1] for m in mods_p), pos_p, lw, tb=tb_p,
                       steps_per_batch=sp // tb_p, batch=1, seq=sp) for b in range(bp)]
        out = _prompt_peer(lw, [(s[2], s[1], s[3]) for s in seqs])
        yp = _final(False, jnp.concatenate([s[0] for s in seqs], axis=0), mods_p[5], out, tb=tb_p,
                    steps_per_batch=sp // tb_p)
        kp_l.append(jnp.stack([s[4][-WINDOW:].reshape(WINDOW, N_KV_HEADS, HEAD_DIM) for s in seqs]))
        vp_l.append(jnp.stack([s[5][-WINDOW:].reshape(WINDOW, N_KV_HEADS, HEAD_DIM) for s in seqs]))
        cp_l.append(jnp.stack([s[6][-(CONV_WIDTH - 1):] for s in seqs]))
        k_new = ksn.reshape(ss, bs, N_KV_HEADS, HEAD_DIM).transpose(1, 0, 2, 3)
        v_new = vsn.reshape(ss, bs, N_KV_HEADS, HEAD_DIM).transpose(1, 0, 2, 3)
        ks_l.append(jnp.concatenate([cache_win_k[l], k_new], axis=1)[:, -WINDOW:])
        vs_l.append(jnp.concatenate([cache_win_v[l], v_new], axis=1)[:, -WINDOW:])
        ext = jnp.concatenate([state_conv[l], us.reshape(ss, bs, D_CONV).transpose(1, 0, 2)], axis=1)
        cs_l.append(ext[:, -(CONV_WIDTH - 1):])
    return (yp.reshape(bp, sp, D_MODEL), ys.reshape(ss, bs, D_MODEL).transpose(1, 0, 2),
            jnp.stack(kp_l), jnp.stack(vp_l), jnp.stack(cp_l), jnp.stack(ks_l), jnp.stack(vs_l), jnp.stack(cs_l))
```

```python
import functools

import numpy as np
import jax
import jax.numpy as jnp
from jax import lax
from jax.experimental import pallas as pl
from jax.experimental.pallas import tpu as pltpu
from jax.experimental.pallas import tpu_sc as plsc

F32 = jnp.float32
BF16 = jnp.bfloat16
HIGHEST = lax.Precision.HIGHEST

D_MODEL = 1024
N_HEADS = 16
HEAD_DIM = 64
N_KV_HEADS = 4
GROUP = N_HEADS // N_KV_HEADS
WINDOW = 128
ATTN_BLOCK = 128
ROPE_THETA = 10000.0
Q_W = N_HEADS * HEAD_DIM
KV_W = N_KV_HEADS * HEAD_DIM
D_CONV = 1024
CONV_WIDTH = 3
PEER_HEADS = 8
N_KEYS = 128
PEER_TOPK = 16
PEER_HALF = 128
PEER_SEL = PEER_HEADS * PEER_TOPK
PAST_LEN = 16384
EPS = 1e-6
NEG = -1e30
IN_SIZES = (Q_W, KV_W, KV_W, D_CONV, D_CONV, D_CONV, D_MODEL, D_MODEL)
IN_OFFS = tuple(int(v) for v in np.cumsum((0,) + IN_SIZES))
D_IN = IN_OFFS[-1]

LANES = 128
SUBLANES = 8
VMEM_LIMIT = 56 * 1024 * 1024
PEER_TOK = 8
POST_TILES = 2
PEER_DMA_QUEUES = 2
SC_CORES = 2
SC_SUBCORES = 16
SC_LANES = 16
SC_WORKERS = SC_CORES * SC_SUBCORES
SC_PASS = 32
SC_TOKEN_SHARE = 0.60
PEER_PAIRS = tuple((i, j) for i in range(PEER_TOPK) for j in range(PEER_TOPK) if (i + 1) * (j + 1) <= PEER_TOPK)


def _const_spec(shape):
    nd = len(shape)
    return pl.BlockSpec(shape, lambda *_: (0,) * nd, pipeline_mode=pl.Buffered(1))


def _params(n_grid):
    return pltpu.CompilerParams(dimension_semantics=("arbitrary",) * n_grid, vmem_limit_bytes=VMEM_LIMIT)


def _mod_kernel(c_ref, w_ref, b_ref, o_ref):
    c = c_ref[...]
    s = c * jax.nn.sigmoid(c)
    o_ref[...] = jnp.dot(s, w_ref[...], precision=HIGHEST, preferred_element_type=F32) + b_ref[...]


def _mod(c, w_ada, b_ada):
    rows, n = c.shape[0], w_ada.shape[1]
    tn = 512
    return pl.pallas_call(
        _mod_kernel,
        grid=(n // tn,),
        in_specs=[pl.BlockSpec((rows, D_MODEL), lambda j: (0, 0)),
                  pl.BlockSpec((D_MODEL, tn), lambda j: (0, j)),
                  pl.BlockSpec((1, tn), lambda j: (0, j))],
        out_specs=pl.BlockSpec((rows, tn), lambda j: (0, j)),
        out_shape=jax.ShapeDtypeStruct((rows, n), F32),
        compiler_params=_params(1),
        name="mod",
    )(c, w_ada, b_ada.reshape(1, n))


def _dot_hi_lo(x, m01):
    hi = x.astype(BF16)
    lo = (x - hi.astype(F32)).astype(BF16)
    m = m01.astype(BF16)
    return jnp.dot(hi, m, preferred_element_type=F32) + jnp.dot(lo, m, preferred_element_type=F32)


def _head_norm(z, g_ref, gt_ref, w):
    ss = _dot_hi_lo(z * z, g_ref[...])
    ssb = _dot_hi_lo(ss, gt_ref[...])
    return z * lax.rsqrt(ssb * (1.0 / HEAD_DIM) + EPS) * w


def _rope(x, cos, sin_signed):
    w = x.shape[1]
    lane = lax.broadcasted_iota(jnp.int32, x.shape, 1)
    fwd = pltpu.roll(x, w - HEAD_DIM // 2, axis=1)
    bwd = pltpu.roll(x, HEAD_DIM // 2, axis=1)
    partner = jnp.where((lane % HEAD_DIM) < HEAD_DIM // 2, fwd, bwd)
    reps = w // LANES
    return x * jnp.tile(cos, (1, reps)) + partner * jnp.tile(sin_signed, (1, reps))


def _in_kernel(sample, tb, steps_per_batch, *refs):
    if sample:
        (x_ref, sh_ref, sc_ref, n1_ref, w_ref, qn_ref, kn_ref, cos_ref, sin_ref, cw_ref,
         gq_ref, gqt_ref, gk_ref, gkt_ref, st_ref,
         q_ref, k_ref, v_ref, bc_ref, ga_ref, gc_ref, u_ref, prev_ref) = refs
    else:
        (x_ref, sh_ref, sc_ref, n1_ref, w_ref, qn_ref, kn_ref, cos_ref, sin_ref, cw_ref,
         gq_ref, gqt_ref, gk_ref, gkt_ref,
         q_ref, k_ref, v_ref, bc_ref, ga_ref, gc_ref, u_ref, ext_ref) = refs
    i = pl.program_id(0)
    x = x_ref[...]
    ms = jnp.mean(x * x, axis=-1, keepdims=True)
    h = (x * lax.rsqrt(ms + EPS) * n1_ref[...]) * (1.0 + sc_ref[...]) + sh_ref[...]
    hb = h.astype(BF16)

    def seg(n):
        return jnp.dot(hb, w_ref[:, IN_OFFS[n]:IN_OFFS[n + 1]], preferred_element_type=F32)

    cos, sin = cos_ref[...], sin_ref[...]
    q = _rope(_head_norm(seg(0), gq_ref, gqt_ref, qn_ref[...]), cos, sin)
    q_ref[...] = (q * (HEAD_DIM ** -0.5)).astype(BF16)
    k_ref[...] = _rope(_head_norm(seg(1), gk_ref, gkt_ref, kn_ref[...]), cos, sin)
    v_ref[...] = seg(2)
    u = seg(4) * seg(5)
    w0, w1, w2 = cw_ref[0:1, :], cw_ref[1:2, :], cw_ref[2:3, :]
    if sample:
        @pl.when(i == 0)
        def _():
            prev_ref[...] = st_ref[...]
        p2, p1 = prev_ref[0], prev_ref[1]
        conv = p2 * w0 + p1 * w1 + u * w2
        prev_ref[0] = p1
        prev_ref[1] = u
        u_ref[...] = u
    else:
        @pl.when(i % steps_per_batch == 0)
        def _():
            ext_ref[0:SUBLANES, :] = jnp.zeros((SUBLANES, D_CONV), F32)

        @pl.when(i % steps_per_batch != 0)
        def _():
            ext_ref[0:SUBLANES, :] = ext_ref[tb:tb + SUBLANES, :]
        ext_ref[SUBLANES:tb + SUBLANES, :] = u
        conv = ext_ref[SUBLANES - 2:tb + SUBLANES - 2, :] * w0 + ext_ref[SUBLANES - 1:tb + SUBLANES - 1, :] * w1 + u * w2
        u_ref[...] = u[tb - SUBLANES:tb, :]
    bc_ref[...] = (seg(3) * conv).astype(BF16)
    ga_ref[...] = jax.nn.sigmoid(seg(6))
    gc_ref[...] = jax.nn.sigmoid(seg(7))


def _group_mats(width):
    lane = np.arange(width)
    g = (lane[:, None] // HEAD_DIM == np.arange(LANES)[None, :]).astype(np.float32)
    return jnp.asarray(g), jnp.asarray(g.T)


def _in_proj(sample, x, sh, sc, norm1, w_in_bf, qn, kn, cos, sin, conv_w, state=None, *, tb, steps_per_batch):
    t = x.shape[0]
    steps = t // tb
    gq, gqt = _group_mats(Q_W)
    gk, gkt = _group_mats(KV_W)
    row = lambda w: pl.BlockSpec((tb, w), lambda i: (i, 0))
    if sample:
        mod_spec = pl.BlockSpec((tb, D_MODEL), lambda i: (0, 0))
        rope_spec = pl.BlockSpec((None, 1, LANES), lambda i: (i, 0, 0))
        u_spec, u_rows = row(D_CONV), t
        scratch = [pltpu.VMEM((2, tb, D_CONV), F32)]
    else:
        mod_spec = pl.BlockSpec((None, 1, D_MODEL), lambda i: (i // steps_per_batch, 0, 0))
        rope_spec = pl.BlockSpec((tb, LANES), lambda i: (i % steps_per_batch, 0))
        u_spec, u_rows = pl.BlockSpec((SUBLANES, D_CONV), lambda i: (i, 0)), steps * SUBLANES
        scratch = [pltpu.VMEM((tb + SUBLANES, D_CONV), F32)]
    in_specs = [row(D_MODEL), mod_spec, mod_spec, _const_spec((1, D_MODEL)), _const_spec((D_MODEL, D_IN)),
                _const_spec((1, Q_W)), _const_spec((1, KV_W)), rope_spec, rope_spec, _const_spec((CONV_WIDTH, D_CONV)),
                _const_spec(gq.shape), _const_spec(gqt.shape), _const_spec(gk.shape), _const_spec(gkt.shape)]
    args = [x, sh, sc, norm1, w_in_bf, qn, kn, cos, sin, conv_w, gq, gqt, gk, gkt]
    if sample:
        in_specs.append(_const_spec(state.shape))
        args.append(state)
    return pl.pallas_call(
        functools.partial(_in_kernel, sample, tb, steps_per_batch),
        grid=(steps,),
        in_specs=in_specs,
        out_specs=[row(Q_W), row(KV_W), row(KV_W), row(D_CONV), row(D_MODEL), row(D_MODEL), u_spec],
        out_shape=[jax.ShapeDtypeStruct((t, Q_W), BF16), jax.ShapeDtypeStruct((t, KV_W), F32),
                   jax.ShapeDtypeStruct((t, KV_W), F32), jax.ShapeDtypeStruct((t, D_CONV), BF16),
                   jax.ShapeDtypeStruct((t, D_MODEL), F32), jax.ShapeDtypeStruct((t, D_MODEL), F32),
                   jax.ShapeDtypeStruct((u_rows, D_CONV), F32)],
        scratch_shapes=scratch,
        compiler_params=_params(1),
        name="in_proj_sample" if sample else "in_proj_prompt",
    )(*args)


def _sink_softmax_pv(q, kg, vg, valid, sink):
    s = lax.dot_general(q, kg, (((1,), (1,)), ((), ())), preferred_element_type=F32)
    s = jnp.where(valid, s, NEG)
    m = jnp.maximum(jnp.max(s, axis=1, keepdims=True), sink)
    p = jnp.exp(s - m)
    denom = jnp.sum(p, axis=1, keepdims=True) + jnp.exp(sink - m)
    return jnp.dot(p.astype(BF16), vg, preferred_element_type=F32) / denom


def _attn_prompt_kernel(sink_ref, q_ref, kc_ref, kp_ref, vc_ref, vp_ref, o_ref):
    j = pl.program_id(1)
    q = q_ref[...]
    kb = jnp.concatenate([kp_ref[...], kc_ref[...]], axis=0).astype(BF16)
    vb = jnp.concatenate([vp_ref[...], vc_ref[...]], axis=0).astype(BF16)
    qi = lax.broadcasted_iota(jnp.int32, (ATTN_BLOCK, 2 * ATTN_BLOCK), 0)
    kj = lax.broadcasted_iota(jnp.int32, (ATTN_BLOCK, 2 * ATTN_BLOCK), 1)
    valid = (kj >= qi) & (kj <= qi + WINDOW) & ((kj >= ATTN_BLOCK) | (j > 0))
    outs = []
    for h in range(N_HEADS):
        g = h // GROUP
        kg = kb[:, g * HEAD_DIM:(g + 1) * HEAD_DIM]
        vg = vb[:, g * HEAD_DIM:(g + 1) * HEAD_DIM]
        outs.append(_sink_softmax_pv(q[:, h * HEAD_DIM:(h + 1) * HEAD_DIM], kg, vg, valid, sink_ref[h]))
    o_ref[...] = jnp.concatenate(outs, axis=1).astype(BF16)


def _attn_prompt(q, k, v, sinks, batch, seq):
    nb = seq // ATTN_BLOCK
    cur = lambda w: pl.BlockSpec((ATTN_BLOCK, w), lambda b, j: (b * nb + j, 0))
    prev = lambda w: pl.BlockSpec((ATTN_BLOCK, w), lambda b, j: (b * nb + jnp.maximum(j - 1, 0), 0))
    return pl.pallas_call(
        _attn_prompt_kernel,
        grid=(batch, nb),
        in_specs=[pl.BlockSpec(memory_space=pltpu.SMEM), cur(Q_W), cur(KV_W), prev(KV_W), cur(KV_W), prev(KV_W)],
        out_specs=cur(Q_W),
        out_shape=jax.ShapeDtypeStruct((batch * seq, Q_W), BF16),
        compiler_params=_params(2),
        name="attn_prompt",
    )(sinks, q, k, k, v, v)


def _attn_sample_kernel(bb, n_new, sink_ref, q_ref, kn_ref, vn_ref, kc_ref, vc_ref, o_ref):
    rows = GROUP * n_new
    s_len = 2 * WINDOW
    ri = lax.broadcasted_iota(jnp.int32, (rows, s_len), 0)
    kj = lax.broadcasted_iota(jnp.int32, (rows, s_len), 1)
    t = ri % n_new
    valid = ((kj < WINDOW) & (kj >= t)) | ((kj >= WINDOW) & (kj - WINDOW <= t))
    pad = jnp.zeros((WINDOW - SUBLANES, KV_W), F32)
    for b in range(bb):
        kb = jnp.concatenate([kc_ref[b], kn_ref[b], pad], axis=0).astype(BF16)
        vb = jnp.concatenate([vc_ref[b], vn_ref[b], pad], axis=0).astype(BF16)
        for g in range(N_KV_HEADS):
            kg = kb[:, g * HEAD_DIM:(g + 1) * HEAD_DIM]
            vg = vb[:, g * HEAD_DIM:(g + 1) * HEAD_DIM]
            o_ref[b, g] = _sink_softmax_pv(q_ref[b, g], kg, vg, valid, sink_ref[:, g:g + 1]).astype(BF16)


def _attn_sample(q4, k_new, v_new, cache_k, cache_v, sink_mat, n_new):
    nbatch = q4.shape[0]
    bb = 8 if nbatch % 8 == 0 else nbatch
    rows = GROUP * n_new
    b3 = lambda r, w: pl.BlockSpec((bb, r, w), lambda i: (i, 0, 0))
    qspec = pl.BlockSpec((bb, N_KV_HEADS, rows, HEAD_DIM), lambda i: (i, 0, 0, 0))
    return pl.pallas_call(
        functools.partial(_attn_sample_kernel, bb, n_new),
        grid=(nbatch // bb,),
        in_specs=[_const_spec(sink_mat.shape), qspec, b3(SUBLANES, KV_W), b3(SUBLANES, KV_W),
                  b3(WINDOW, KV_W), b3(WINDOW, KV_W)],
        out_specs=qspec,
        out_shape=jax.ShapeDtypeStruct(q4.shape, BF16),
        compiler_params=_params(1),
        name="attn_sample",
    )(sink_mat, q4, k_new, v_new, cache_k, cache_v)


def _topk_rows(s, k, payload=None):
    r = s.shape[0]
    rows = lax.broadcasted_iota(jnp.int32, s.shape, 0)
    vals, sel = [], []
    for _ in range(k):
        m = jnp.max(s, axis=0, keepdims=True)
        ix = jnp.min(jnp.where(s == m, rows, r), axis=0, keepdims=True)
        hit = rows == ix
        vals.append(m)
        sel.append(ix if payload is None else jnp.max(jnp.where(hit, payload, -1), axis=0, keepdims=True))
        s = jnp.where(hit, -jnp.inf, s)
    return jnp.concatenate(vals, axis=0), jnp.concatenate(sel, axis=0)


def _peer_route(q0, q1, k0, k1):
    nt = (((1,), (1,)), ((), ()))
    s0 = lax.dot_general(k0, q0, nt, preferred_element_type=F32)
    s1 = lax.dot_general(k1, q1, nt, preferred_element_type=F32)
    sv0, si0 = _topk_rows(s0, PEER_TOPK)
    sv1, si1 = _topk_rows(s1, PEER_TOPK)
    tb = s0.shape[1]
    row8 = lax.broadcasted_iota(jnp.int32, (SUBLANES, tb), 0)
    n_grp = -(-len(PEER_PAIRS) // SUBLANES)
    cvs, ces = [], []
    for gi in range(n_grp):
        cv = jnp.full((SUBLANES, tb), -jnp.inf, F32)
        ce = jnp.zeros((SUBLANES, tb), jnp.int32)
        for r in range(SUBLANES):
            c = gi * SUBLANES + r
            if c < len(PEER_PAIRS):
                i, j = PEER_PAIRS[c]
                cv = jnp.where(row8 == r, sv0[i:i + 1, :] + sv1[j:j + 1, :], cv)
                ce = jnp.where(row8 == r, si0[i:i + 1, :] * N_KEYS + si1[j:j + 1, :], ce)
        cvs.append(cv)
        ces.append(ce)
    cs, e = _topk_rows(jnp.concatenate(cvs, axis=0), PEER_TOPK, payload=jnp.concatenate(ces, axis=0))
    p = jnp.exp(cs - cs[0:1, :])
    return e, p / jnp.sum(p, axis=0, keepdims=True)


def _post_kernel(tb, x_ref, at_ref, bc_ref, ga_ref, gc_ref, gt1_ref, sh2_ref, sc2_ref, n2_ref,
                 wa_ref, wc_ref, wo_ref, wpq_ref, sk_ref, x1_ref, h2_ref, e_ref, g_ref, qp_ref):
    merged = (ga_ref[...] * jnp.dot(at_ref[...], wa_ref[...], preferred_element_type=F32)
              + gc_ref[...] * jnp.dot(bc_ref[...], wc_ref[...], preferred_element_type=F32))
    x1 = x_ref[...] + gt1_ref[...] * jnp.dot(merged.astype(BF16), wo_ref[...], preferred_element_type=F32)
    x1_ref[...] = x1
    ms = jnp.mean(x1 * x1, axis=-1, keepdims=True)
    h2 = (x1 * lax.rsqrt(ms + EPS) * n2_ref[...]) * (1.0 + sc2_ref[...]) + sh2_ref[...]
    h2_ref[...] = h2
    qp = jnp.dot(h2.astype(BF16), wpq_ref[...], preferred_element_type=F32).astype(BF16)
    for hp in range(2 * PEER_HEADS):
        qp_ref[hp] = qp[:, hp * PEER_HALF:(hp + 1) * PEER_HALF]

    def head(h, carry):
        e, g = _peer_route(qp_ref[2 * h], qp_ref[2 * h + 1], sk_ref[2 * h], sk_ref[2 * h + 1])
        e_ref[h] = e
        g_ref[h] = g
        return carry

    lax.fori_loop(0, PEER_HEADS, head, 0)


def _post(sample, x, attn, bc, ga, gc, gt1, sh2, sc2, norm2, wa, wc, wo, wpq, sk, *, tb, steps_per_batch):
    t = x.shape[0]
    row = lambda w: pl.BlockSpec((tb, w), lambda i: (i, 0))
    if sample:
        mod_spec = pl.BlockSpec((tb, D_MODEL), lambda i: (0, 0))
    else:
        mod_spec = pl.BlockSpec((None, 1, D_MODEL), lambda i: (i // steps_per_batch, 0, 0))
    sel_spec = pl.BlockSpec((PEER_HEADS, PEER_TOPK, tb), lambda i: (0, 0, i))
    return pl.pallas_call(
        functools.partial(_post_kernel, tb),
        grid=(t // tb,),
        in_specs=[row(D_MODEL), row(Q_W), row(D_CONV), row(D_MODEL), row(D_MODEL), mod_spec, mod_spec, mod_spec,
                  _const_spec((1, D_MODEL)), _const_spec(wa.shape), _const_spec(wc.shape), _const_spec(wo.shape),
                  _const_spec(wpq.shape), _const_spec(sk.shape)],
        out_specs=[row(D_MODEL), row(D_MODEL), sel_spec, sel_spec],
        out_shape=[jax.ShapeDtypeStruct((t, D_MODEL), F32), jax.ShapeDtypeStruct((t, D_MODEL), F32),
                   jax.ShapeDtypeStruct((PEER_HEADS, PEER_TOPK, t), jnp.int32),
                   jax.ShapeDtypeStruct((PEER_HEADS, PEER_TOPK, t), F32)],
        scratch_shapes=[pltpu.VMEM((2 * PEER_HEADS, tb, PEER_HALF), BF16)],
        compiler_params=_params(1),
        name="post_sample" if sample else "post_prompt",
    )(x, attn, bc, ga, gc, gt1, sh2, sc2, norm2, wa, wc, wo, wpq, sk)


def _sublane_sums(p):
    sub = lax.broadcasted_iota(jnp.int32, (SUBLANES, LANES), 0)

    def merge(a, b, sh):
        lo = (sub & sh) == 0
        fa = a + pltpu.roll(a, SUBLANES - sh, axis=0)
        fb = b + pltpu.roll(b, sh, axis=0)
        return jnp.where(lo, fa, fb)

    n = [merge(p[i], p[i + 4], 4) for i in range(4)]
    q = [merge(n[0], n[2], 2), merge(n[1], n[3], 2)]
    return merge(q[0], q[1], 1)


def _u_half(word):
    return lax.bitcast_convert_type(word & jnp.int32(-65536), F32)


def _v_half(word):
    return lax.bitcast_convert_type(word << 16, F32)


def _peer_kernel(n, idx_cur, idx_nxt, h_ref, g_ref, uv_hbm, o_ref, buf0, buf1, wbuf, sem):
    i = pl.program_id(0)
    phase_rows = PEER_TOK * PEER_SEL
    bufs = (buf0, buf1)

    def token_rows(s, t):
        return bufs[s].at[pl.ds(t * PEER_SEL, PEER_SEL)]

    def issue_token(idx_ref, off, s, t):
        for k in range(PEER_SEL):
            r = t * PEER_SEL + k
            pltpu.make_async_copy(uv_hbm.at[idx_ref[off + r]], bufs[s].at[r], sem.at[s, t]).start(
                priority=k % PEER_DMA_QUEUES)

    def wait_token(s, t):
        pltpu.make_async_copy(uv_hbm.at[pl.ds(0, PEER_SEL)], token_rows(s, t), sem.at[s, t]).wait()

    @pl.when(i == 0)
    def _():
        def body(r, c):
            pltpu.make_async_copy(uv_hbm.at[idx_cur[r]], buf0.at[r], sem.at[0, r // PEER_SEL]).start()
            return c
        lax.fori_loop(0, phase_rows, body, 0)

    eye = (lax.broadcasted_iota(jnp.int32, (PEER_SEL, PEER_SEL), 0)
           == lax.broadcasted_iota(jnp.int32, (PEER_SEL, PEER_SEL), 1)).astype(F32)
    gcol = lax.dot_general(eye, g_ref[...], (((1,), (1,)), ((), ())), precision=HIGHEST, preferred_element_type=F32)
    for s in range(2):
        for t in range(PEER_TOK):
            wait_token(s, t)
            if s == 0:
                issue_token(idx_cur, phase_rows, 1, t)
            else:
                issue_token(idx_nxt, 0, 0, t)
            tok = s * PEER_TOK + t
            h = h_ref[tok]
            base = t * PEER_SEL
            parts = []
            for k0 in range(0, PEER_SEL, SUBLANES):
                parts.append(_sublane_sums([_u_half(bufs[s][base + k0 + j]) * h for j in range(SUBLANES)]))
            a = jnp.sum(jnp.concatenate(parts, axis=0), axis=1, keepdims=True)
            w = gcol[:, tok:tok + 1] * jax.nn.gelu(a)
            wbuf[...] = jnp.broadcast_to(w, (PEER_SEL, LANES))
            accs = [jnp.zeros((SUBLANES, LANES), F32) for _ in range(4)]
            for k in range(PEER_SEL):
                accs[k % 4] = accs[k % 4] + wbuf[k:k + 1, :] * _v_half(bufs[s][base + k])
            o_ref[tok] = (accs[0] + accs[1]) + (accs[2] + accs[3])

    @pl.when(i == n - 1)
    def _():
        for t in range(PEER_TOK):
            wait_token(0, t)


def _peer(idx_flat, h3, g_tok, uv3):
    t = h3.shape[0]
    step_tok = 2 * PEER_TOK
    n = t // step_tok
    rows = step_tok * PEER_SEL
    tile = pl.BlockSpec((step_tok, SUBLANES, LANES), lambda i: (i, 0, 0))
    return pl.pallas_call(
        functools.partial(_peer_kernel, n),
        grid=(n,),
        in_specs=[pl.BlockSpec((rows,), lambda i: (i,), memory_space=pltpu.SMEM),
                  pl.BlockSpec((rows,), lambda i: (jnp.minimum(i + 1, n - 1),), memory_space=pltpu.SMEM),
                  tile,
                  pl.BlockSpec((step_tok, PEER_SEL), lambda i: (i, 0)),
                  pl.BlockSpec(memory_space=pl.ANY)],
        out_specs=tile,
        out_shape=jax.ShapeDtypeStruct((t, SUBLANES, LANES), F32),
        scratch_shapes=[pltpu.VMEM((PEER_TOK * PEER_SEL, SUBLANES, LANES), jnp.int32),
                        pltpu.VMEM((PEER_TOK * PEER_SEL, SUBLANES, LANES), jnp.int32),
                        pltpu.VMEM((PEER_SEL, LANES), F32),
                        pltpu.SemaphoreType.DMA((2, PEER_TOK))],
        compiler_params=pltpu.CompilerParams(dimension_semantics=("arbitrary",), vmem_limit_bytes=VMEM_LIMIT,
                                             disable_bounds_checks=True),
        name="peer_mix",
    )(idx_flat, idx_flat, h3, g_tok, uv3)


def _gelu_tanh_via_exp(a):
    z = 0.7978845608028654 * (a + 0.044715 * (a * a * a))
    t = 1.0 - 2.0 / (jnp.exp(2.0 * z) + 1.0)
    return 0.5 * a * (1.0 + t)


def _sc_peer(tab, idx, h, g):
    t_all = idx.shape[0]
    per_w = t_all // SC_WORKERS
    n_pass = PEER_SEL // SC_PASS
    n_chunk = D_MODEL // SC_LANES
    quarter = 16 * SC_LANES
    mesh = plsc.VectorSubcoreMesh(core_axis_name="c", subcore_axis_name="s")

    @functools.partial(
        pl.kernel, mesh=mesh,
        out_type=jax.ShapeDtypeStruct((t_all, D_MODEL), F32),
        scratch_types=[pltpu.VMEM((2, PEER_SEL), jnp.int32),
                       pltpu.VMEM((2, SC_PASS, SUBLANES, LANES), jnp.int32),
                       pltpu.VMEM((2, D_MODEL), F32),
                       pltpu.VMEM((2, PEER_SEL), F32),
                       pltpu.VMEM((SC_PASS, SC_LANES), F32),
                       pltpu.VMEM((PEER_SEL,), F32),
                       pltpu.VMEM((D_MODEL,), F32),
                       pltpu.SemaphoreType.DMA((2,)),
                       pltpu.SemaphoreType.DMA((2,))],
        compiler_params=pltpu.CompilerParams(needs_layout_passes=False),
    )
    def k(tab_hbm, idx_hbm, h_hbm, g_hbm, out_hbm, idx_v, rows_v, h_v, g_v, ap_v, w_v, o_v, tsem, rsem):
        wid = lax.axis_index("s") * SC_CORES + lax.axis_index("c")
        lane = lax.iota(jnp.int32, SC_LANES)
        zero = jnp.zeros((SC_LANES,), F32)
        tok0 = wid * per_w

        def token_inputs(tok, slot):
            return (pltpu.make_async_copy(idx_hbm.at[tok], idx_v.at[slot], tsem.at[slot]),
                    pltpu.make_async_copy(h_hbm.at[tok], h_v.at[slot], tsem.at[slot]),
                    pltpu.make_async_copy(g_hbm.at[tok], g_v.at[slot], tsem.at[slot]))

        def gather(slot, q, b):
            ids = idx_v.at[slot].at[pl.ds(q * SC_PASS, SC_PASS)]
            return pltpu.make_async_copy(tab_hbm.at[ids], rows_v.at[b], rsem.at[b])

        def row_chunk(b, kk, jc, j):
            col = jc * quarter + j * SC_LANES
            return rows_v[b, kk, col // LANES, pl.ds(col % LANES, SC_LANES)]

        for c in token_inputs(tok0, 0):
            c.start()
        for c in token_inputs(tok0, 0):
            c.wait()
        gather(0, 0, 0).start()

        def token(ti, carry):
            ts = lax.rem(ti, 2)
            tn = 1 - ts
            tok = tok0 + ti
            nxt = tok0 + jnp.minimum(ti + 1, per_w - 1)
            for c in token_inputs(nxt, tn):
                c.start()
            for j in range(n_chunk):
                o_v[pl.ds(j * SC_LANES, SC_LANES)] = zero
            for q in range(n_pass):
                b = q % 2
                gather(ts, q, b).wait()
                if q + 1 < n_pass:
                    gather(ts, q + 1, 1 - b).start()
                else:
                    for c in token_inputs(nxt, tn):
                        c.wait()
                    gather(tn, 0, 1 - b).start()
                for jc in range(D_MODEL // quarter):
                    hregs = [h_v[ts, pl.ds(jc * quarter + j * SC_LANES, SC_LANES)] for j in range(16)]

                    def dot_rows(kk, jc=jc, hregs=hregs, b=b):
                        part = [None] * 4
                        for j in range(16):
                            p = _u_half(row_chunk(b, kk, jc, j)) * hregs[j]
                            part[j % 4] = p if part[j % 4] is None else part[j % 4] + p
                        acc = (part[0] + part[1]) + (part[2] + part[3])
                        if jc == 0:
                            ap_v[kk, :] = acc
                        else:
                            plsc.addupdate(ap_v.at[kk], acc)

                    plsc.parallel_loop(0, SC_PASS)(dot_rows)
                for m in range(SC_PASS // SC_LANES):
                    rowi = lane + m * SC_LANES
                    a = zero
                    for col in range(SC_LANES):
                        a = a + plsc.load_gather(ap_v, [rowi, jnp.full((SC_LANES,), col, jnp.int32)])
                    off = q * SC_PASS + m * SC_LANES
                    w_v[pl.ds(off, SC_LANES)] = g_v[ts, pl.ds(off, SC_LANES)] * _gelu_tanh_via_exp(a)
                for jc in range(D_MODEL // quarter):
                    def mix_rows(kk, accs, q=q, jc=jc, b=b):
                        ws = plsc.load_gather(w_v, [jnp.full((SC_LANES,), q * SC_PASS, jnp.int32) + kk])
                        return tuple(accs[j] + ws * _v_half(row_chunk(b, kk, jc, j)) for j in range(16))

                    accs = plsc.parallel_loop(0, SC_PASS, carry=tuple(zero for _ in range(16)))(mix_rows)
                    for j in range(16):
                        plsc.addupdate(o_v.at[pl.ds(jc * quarter + j * SC_LANES, SC_LANES)], accs[j])
            pltpu.sync_copy(o_v, out_hbm.at[tok])
            return carry

        lax.fori_loop(0, per_w, token, 0)
        gather(0, 0, 0).wait()

    return k(tab, idx, h, g)


def _final_kernel(x_ref, gt_ref, o_ref, y_ref):
    y_ref[...] = x_ref[...] + gt_ref[...] * o_ref[...]


def _final(sample, x1, gt2, out, *, tb, steps_per_batch):
    t = x1.shape[0]
    row = pl.BlockSpec((tb, D_MODEL), lambda i: (i, 0))
    if sample:
        mod_spec = pl.BlockSpec((tb, D_MODEL), lambda i: (0, 0))
    else:
        mod_spec = pl.BlockSpec((None, 1, D_MODEL), lambda i: (i // steps_per_batch, 0, 0))
    return pl.pallas_call(
        _final_kernel,
        grid=(t // tb,),
        in_specs=[row, mod_spec, row],
        out_specs=row,
        out_shape=jax.ShapeDtypeStruct((t, D_MODEL), F32),
        compiler_params=_params(1),
        name="final_sample" if sample else "final_prompt",
    )(x1, gt2, out)


def _rope_tables(pos):
    half = HEAD_DIM // 2
    inv = jnp.power(jnp.float32(ROPE_THETA), -jnp.arange(half, dtype=F32) / half)
    ang = pos.astype(F32)[:, None] * inv[None, :]
    lane = np.arange(LANES)
    sign = jnp.asarray(np.where((lane % HEAD_DIM) < half, -1.0, 1.0).astype(np.float32))
    cos = jnp.cos(ang)[:, lane % half]
    sin = jnp.sin(ang)[:, lane % half] * sign[None, :]
    return cos, sin


def _dense(sample, x, mods, pos, lw, *, tb, steps_per_batch, batch, seq, win_k=None, win_v=None, conv_state=None):
    sh1, sc1, gt1, sh2, sc2, _ = mods
    cos, sin = _rope_tables(pos)
    if sample:
        cos, sin = cos[:, None, :], sin[:, None, :]
    q, k, v, bc, ga, gc, u_out = _in_proj(sample, x, sh1, sc1, lw["norm1"], lw["w_in"], lw["qn"], lw["kn"], cos, sin,
                                          lw["conv_w"], conv_state, tb=tb, steps_per_batch=steps_per_batch)
    if sample:
        n_new, nb = seq, batch
        q4 = q.reshape(n_new, nb, N_KV_HEADS, GROUP, HEAD_DIM).transpose(1, 2, 3, 0, 4).reshape(
            nb, N_KV_HEADS, GROUP * n_new, HEAD_DIM)
        to_b = lambda a: jnp.pad(a.reshape(n_new, nb, KV_W).transpose(1, 0, 2), ((0, 0), (0, SUBLANES - n_new), (0, 0)))
        sink_mat = jnp.repeat(lw["sinks"].reshape(N_KV_HEADS, GROUP), n_new, axis=1).T
        o4 = _attn_sample(q4, to_b(k), to_b(v), win_k.reshape(nb, WINDOW, KV_W), win_v.reshape(nb, WINDOW, KV_W),
                          sink_mat, n_new)
        attn = o4.reshape(nb, N_KV_HEADS, GROUP, n_new, HEAD_DIM).transpose(3, 0, 1, 2, 4).reshape(n_new * nb, Q_W)
    else:
        attn = _attn_prompt(q, k, v, lw["sinks"], batch, seq)
    pt = 1 if sample or steps_per_batch % POST_TILES else POST_TILES
    x1, h2, e, g = _post(sample, x, attn, bc, ga, gc, gt1, sh2, sc2, lw["norm2"], lw["wa"], lw["wc"], lw["wo"],
                         lw["wpq"], lw["sk"], tb=tb * pt, steps_per_batch=steps_per_batch // pt)
    t = x.shape[0]
    return x1, h2, e.reshape(PEER_SEL, t).T, g.reshape(PEER_SEL, t).T, k, v, u_out


def _tc_peer(lw, idx, h2, g_tok):
    t = idx.shape[0]
    return _peer(idx.reshape(t * PEER_SEL), h2.reshape(t, SUBLANES, LANES), g_tok,
                 lw["uv"]).reshape(t, D_MODEL)


def _prompt_peer(lw, parts):
    unit = 2 * SC_WORKERS
    total = sum(p[0].shape[0] for p in parts)
    left = int(total * SC_TOKEN_SHARE) // unit * unit
    n_sc = [0] * len(parts)
    for i in reversed(range(len(parts))):
        n_sc[i] = min(left, parts[i][0].shape[0])
        left -= n_sc[i]
    sc_out = [_sc_peer(lw["uv"], idx[:n], h2[:n], g_tok[:n]) if n else None
              for (idx, h2, g_tok), n in zip(parts, n_sc)]
    outs = []
    for (idx, h2, g_tok), n, so in zip(parts, n_sc, sc_out):
        if so is not None:
            outs.append(so)
        if n < idx.shape[0]:
            outs.append(_tc_peer(lw, idx[n:], h2[n:], g_tok[n:]))
    return jnp.concatenate(outs, axis=0)


def _pack_uv(u, v):
    ub = lax.bitcast_convert_type(u.astype(BF16), jnp.uint16).astype(jnp.uint32)
    vb = lax.bitcast_convert_type(v.astype(BF16), jnp.uint16).astype(jnp.uint32)
    return lax.bitcast_convert_type((ub << 16) | vb, jnp.int32).reshape(-1, SUBLANES, LANES)


def kernel(x_prompt, x_sample, c_prompt, c_sample, cache_win_k, cache_win_v, state_conv, w_ada, b_ada, norm1, w_in, q_norm, k_norm, attn_sinks, conv_w, w_attn_o, w_conv_o, w_out, norm2, w_peer_q, peer_sub_keys, peer_u, peer_v):
    depth = w_ada.shape[0]
    bp, sp, _ = x_prompt.shape
    bs, ss, _ = x_sample.shape
    assert sp % ATTN_BLOCK == 0 and ss <= SUBLANES and (bs * ss) % (2 * PEER_TOK) == 0
    tb_p = 256 if sp % 256 == 0 else ATTN_BLOCK
    pos_p = jnp.arange(sp)
    pos_s = PAST_LEN + jnp.arange(ss)
    yp = x_prompt.reshape(bp * sp, D_MODEL)
    ys = x_sample.transpose(1, 0, 2).reshape(ss * bs, D_MODEL)
    kp_l, vp_l, cp_l, ks_l, vs_l, cs_l = [], [], [], [], [], []
    for l in range(depth):
        lw = {
            "norm1": norm1[l].reshape(1, D_MODEL), "norm2": norm2[l].reshape(1, D_MODEL),
            "w_in": w_in[l].astype(BF16),
            "qn": jnp.tile(q_norm[l], N_HEADS).reshape(1, Q_W), "kn": jnp.tile(k_norm[l], N_KV_HEADS).reshape(1, KV_W),
            "sinks": attn_sinks[l], "conv_w": conv_w[l],
            "wa": w_attn_o[l].astype(BF16), "wc": w_conv_o[l].astype(BF16), "wo": w_out[l].astype(BF16),
            "wpq": w_peer_q[l].astype(BF16),
            "sk": peer_sub_keys[l].reshape(2 * PEER_HEADS, N_KEYS, PEER_HALF).astype(BF16),
            "uv": _pack_uv(peer_u[l], peer_v[l]),
        }
        c_all = jnp.concatenate([c_prompt, c_sample], axis=0)
        pad = (-c_all.shape[0]) % SUBLANES
        mod = _mod(jnp.pad(c_all, ((0, pad), (0, 0))), w_ada[l], b_ada[l])
        mods_p = tuple(m.reshape(bp, 1, D_MODEL) for m in jnp.split(mod[:bp], 6, axis=-1))
        mods_s = tuple(jnp.split(mod[bp:bp + bs], 6, axis=-1))
        x1, h2, idx, g_tok, ksn, vsn, us = _dense(True, ys, mods_s, pos_s, lw, tb=bs, steps_per_batch=1, batch=bs,
                                                  seq=ss, win_k=cache_win_k[l], win_v=cache_win_v[l],
                                                  conv_state=state_conv[l].transpose(1, 0, 2))
        ys = _final(True, x1, mods_s[5], _tc_peer(lw, idx, h2, g_tok), tb=bs, steps_per_batch=1)
        seqs = [_dense(False, yp[b * sp:(b + 1) * sp], tuple(m[b:b + 1] for m in mods_p), pos_p, lw, tb=tb_p,
                       steps_per_batch=sp // tb_p, batch=1, seq=sp) for b in range(bp)]
        out = _prompt_peer(lw, [(s[2], s[1], s[3]) for s in seqs])
        yp = _final(False, jnp.concatenate([s[0] for s in seqs], axis=0), mods_p[5], out, tb=tb_p,
                    steps_per_batch=sp // tb_p)
        kp_l.append(jnp.stack([s[4][-WINDOW:].reshape(WINDOW, N_KV_HEADS, HEAD_DIM) for s in seqs]))
        vp_l.append(jnp.stack([s[5][-WINDOW:].reshape(WINDOW, N_KV_HEADS, HEAD_DIM) for s in seqs]))
        cp_l.append(jnp.stack([s[6][-(CONV_WIDTH - 1):] for s in seqs]))
        k_new = ksn.reshape(ss, bs, N_KV_HEADS, HEAD_DIM).transpose(1, 0, 2, 3)
        v_new = vsn.reshape(ss, bs, N_KV_HEADS, HEAD_DIM).transpose(1, 0, 2, 3)
        ks_l.append(jnp.concatenate([cache_win_k[l], k_new], axis=1)[:, -WINDOW:])
        vs_l.append(jnp.concatenate([cache_win_v[l], v_new], axis=1)[:, -WINDOW:])
        ext = jnp.concatenate([state_conv[l], us.reshape(ss, bs, D_CONV).transpose(1, 0, 2)], axis=1)
        cs_l.append(ext[:, -(CONV_WIDTH - 1):])
    return (yp.reshape(bp, sp, D_MODEL), ys.reshape(ss, bs, D_MODEL).transpose(1, 0, 2),
            jnp.stack(kp_l), jnp.stack(vp_l), jnp.stack(cp_l), jnp.stack(ks_l), jnp.stack(vs_l), jnp.stack(cs_l))
```

```python
import functools

import numpy as np
import jax
import jax.numpy as jnp
from jax import lax
from jax.experimental import pallas as pl
from jax.experimental.pallas import tpu as pltpu
from jax.experimental.pallas import tpu_sc as plsc

F32 = jnp.float32
BF16 = jnp.bfloat16
HIGHEST = lax.Precision.HIGHEST

D_MODEL = 1024
N_HEADS = 16
HEAD_DIM = 64
N_KV_HEADS = 4
GROUP = N_HEADS // N_KV_HEADS
WINDOW = 128
ATTN_BLOCK = 128
ROPE_THETA = 10000.0
Q_W = N_HEADS * HEAD_DIM
KV_W = N_KV_HEADS * HEAD_DIM
D_CONV = 1024
CONV_WIDTH = 3
PEER_HEADS = 8
N_KEYS = 128
PEER_TOPK = 16
PEER_HALF = 128
PEER_SEL = PEER_HEADS * PEER_TOPK
PAST_LEN = 16384
EPS = 1e-6
NEG = -1e30
IN_SIZES = (Q_W, KV_W, KV_W, D_CONV, D_CONV, D_CONV, D_MODEL, D_MODEL)
IN_OFFS = tuple(int(v) for v in np.cumsum((0,) + IN_SIZES))
D_IN = IN_OFFS[-1]

LANES = 128
SUBLANES = 8
VMEM_LIMIT = 56 * 1024 * 1024
PEER_TOK = 8
POST_TILES = 2
PEER_DMA_QUEUES = 2
SC_CORES = 2
SC_SUBCORES = 16
SC_LANES = 16
SC_WORKERS = SC_CORES * SC_SUBCORES
SC_PASS = 32
SC_TOKEN_SHARE = 0.60
PEER_PAIRS = tuple((i, j) for i in range(PEER_TOPK) for j in range(PEER_TOPK) if (i + 1) * (j + 1) <= PEER_TOPK)


def _const_spec(shape):
    nd = len(shape)
    return pl.BlockSpec(shape, lambda *_: (0,) * nd, pipeline_mode=pl.Buffered(1))


def _params(n_grid):
    return pltpu.CompilerParams(dimension_semantics=("arbitrary",) * n_grid, vmem_limit_bytes=VMEM_LIMIT)


def _mod_kernel(c_ref, w_ref, b_ref, o_ref):
    c = c_ref[...]
    s = c * jax.nn.sigmoid(c)
    o_ref[...] = jnp.dot(s, w_ref[...], precision=HIGHEST, preferred_element_type=F32) + b_ref[...]


def _mod(c, w_ada, b_ada):
    rows, n = c.shape[0], w_ada.shape[1]
    tn = 512
    return pl.pallas_call(
        _mod_kernel,
        grid=(n // tn,),
        in_specs=[pl.BlockSpec((rows, D_MODEL), lambda j: (0, 0)),
                  pl.BlockSpec((D_MODEL, tn), lambda j: (0, j)),
                  pl.BlockSpec((1, tn), lambda j: (0, j))],
        out_specs=pl.BlockSpec((rows, tn), lambda j: (0, j)),
        out_shape=jax.ShapeDtypeStruct((rows, n), F32),
        compiler_params=_params(1),
        name="mod",
    )(c, w_ada, b_ada.reshape(1, n))


def _dot_hi_lo(x, m01):
    hi = x.astype(BF16)
    lo = (x - hi.astype(F32)).astype(BF16)
    m = m01.astype(BF16)
    return jnp.dot(hi, m, preferred_element_type=F32) + jnp.dot(lo, m, preferred_element_type=F32)


def _head_norm(z, g_ref, gt_ref, w):
    ss = _dot_hi_lo(z * z, g_ref[...])
    ssb = _dot_hi_lo(ss, gt_ref[...])
    return z * lax.rsqrt(ssb * (1.0 / HEAD_DIM) + EPS) * w


def _rope(x, cos, sin_signed):
    w = x.shape[1]
    lane = lax.broadcasted_iota(jnp.int32, x.shape, 1)
    fwd = pltpu.roll(x, w - HEAD_DIM // 2, axis=1)
    bwd = pltpu.roll(x, HEAD_DIM // 2, axis=1)
    partner = jnp.where((lane % HEAD_DIM) < HEAD_DIM // 2, fwd, bwd)
    reps = w // LANES
    return x * jnp.tile(cos, (1, reps)) + partner * jnp.tile(sin_signed, (1, reps))


def _in_kernel(sample, tb, steps_per_batch, *refs):
    if sample:
        (x_ref, sh_ref, sc_ref, n1_ref, w_ref, qn_ref, kn_ref, cos_ref, sin_ref, cw_ref,
         gq_ref, gqt_ref, gk_ref, gkt_ref, st_ref,
         q_ref, k_ref, v_ref, bc_ref, ga_ref, gc_ref, u_ref, prev_ref) = refs
    else:
        (x_ref, sh_ref, sc_ref, n1_ref, w_ref, qn_ref, kn_ref, cos_ref, sin_ref, cw_ref,
         gq_ref, gqt_ref, gk_ref, gkt_ref,
         q_ref, k_ref, v_ref, bc_ref, ga_ref, gc_ref, u_ref, ext_ref) = refs
    i = pl.program_id(0)
    x = x_ref[...]
    ms = jnp.mean(x * x, axis=-1, keepdims=True)
    h = (x * lax.rsqrt(ms + EPS) * n1_ref[...]) * (1.0 + sc_ref[...]) + sh_ref[...]
    hb = h.astype(BF16)

    def seg(n):
        return jnp.dot(hb, w_ref[:, IN_OFFS[n]:IN_OFFS[n + 1]], preferred_element_type=F32)

    cos, sin = cos_ref[...], sin_ref[...]
    q = _rope(_head_norm(seg(0), gq_ref, gqt_ref, qn_ref[...]), cos, sin)
    q_ref[...] = (q * (HEAD_DIM ** -0.5)).astype(BF16)
    k_ref[...] = _rope(_head_norm(seg(1), gk_ref, gkt_ref, kn_ref[...]), cos, sin)
    v_ref[...] = seg(2)
    u = seg(4) * seg(5)
    w0, w1, w2 = cw_ref[0:1, :], cw_ref[1:2, :], cw_ref[2:3, :]
    if sample:
        @pl.when(i == 0)
        def _():
            prev_ref[...] = st_ref[...]
        p2, p1 = prev_ref[0], prev_ref[1]
        conv = p2 * w0 + p1 * w1 + u * w2
        prev_ref[0] = p1
        prev_ref[1] = u
        u_ref[...] = u
    else:
        @pl.when(i % steps_per_batch == 0)
        def _():
            ext_ref[0:SUBLANES, :] = jnp.zeros((SUBLANES, D_CONV), F32)

        @pl.when(i % steps_per_batch != 0)
        def _():
            ext_ref[0:SUBLANES, :] = ext_ref[tb:tb + SUBLANES, :]
        ext_ref[SUBLANES:tb + SUBLANES, :] = u
        conv = ext_ref[SUBLANES - 2:tb + SUBLANES - 2, :] * w0 + ext_ref[SUBLANES - 1:tb + SUBLANES - 1, :] * w1 + u * w2
        u_ref[...] = u[tb - SUBLANES:tb, :]
    bc_ref[...] = (seg(3) * conv).astype(BF16)
    ga_ref[...] = jax.nn.sigmoid(seg(6))
    gc_ref[...] = jax.nn.sigmoid(seg(7))


def _group_mats(width):
    lane = np.arange(width)
    g = (lane[:, None] // HEAD_DIM == np.arange(LANES)[None, :]).astype(np.float32)
    return jnp.asarray(g), jnp.asarray(g.T)


def _in_proj(sample, x, sh, sc, norm1, w_in_bf, qn, kn, cos, sin, conv_w, state=None, *, tb, steps_per_batch):
    t = x.shape[0]
    steps = t // tb
    gq, gqt = _group_mats(Q_W)
    gk, gkt = _group_mats(KV_W)
    row = lambda w: pl.BlockSpec((tb, w), lambda i: (i, 0))
    if sample:
        mod_spec = pl.BlockSpec((tb, D_MODEL), lambda i: (0, 0))
        rope_spec = pl.BlockSpec((None, 1, LANES), lambda i: (i, 0, 0))
        u_spec, u_rows = row(D_CONV), t
        scratch = [pltpu.VMEM((2, tb, D_CONV), F32)]
    else:
        mod_spec = pl.BlockSpec((None, 1, D_MODEL), lambda i: (i // steps_per_batch, 0, 0))
        rope_spec = pl.BlockSpec((tb, LANES), lambda i: (i % steps_per_batch, 0))
        u_spec, u_rows = pl.BlockSpec((SUBLANES, D_CONV), lambda i: (i, 0)), steps * SUBLANES
        scratch = [pltpu.VMEM((tb + SUBLANES, D_CONV), F32)]
    in_specs = [row(D_MODEL), mod_spec, mod_spec, _const_spec((1, D_MODEL)), _const_spec((D_MODEL, D_IN)),
                _const_spec((1, Q_W)), _const_spec((1, KV_W)), rope_spec, rope_spec, _const_spec((CONV_WIDTH, D_CONV)),
                _const_spec(gq.shape), _const_spec(gqt.shape), _const_spec(gk.shape), _const_spec(gkt.shape)]
    args = [x, sh, sc, norm1, w_in_bf, qn, kn, cos, sin, conv_w, gq, gqt, gk, gkt]
    if sample:
        in_specs.append(_const_spec(state.shape))
        args.append(state)
    return pl.pallas_call(
        functools.partial(_in_kernel, sample, tb, steps_per_batch),
        grid=(steps,),
        in_specs=in_specs,
        out_specs=[row(Q_W), row(KV_W), row(KV_W), row(D_CONV), row(D_MODEL), row(D_MODEL), u_spec],
        out_shape=[jax.ShapeDtypeStruct((t, Q_W), BF16), jax.ShapeDtypeStruct((t, KV_W), F32),
                   jax.ShapeDtypeStruct((t, KV_W), F32), jax.ShapeDtypeStruct((t, D_CONV), BF16),
                   jax.ShapeDtypeStruct((t, D_MODEL), F32), jax.ShapeDtypeStruct((t, D_MODEL), F32),
                   jax.ShapeDtypeStruct((u_rows, D_CONV), F32)],
        scratch_shapes=scratch,
        compiler_params=_params(1),
        name="in_proj_sample" if sample else "in_proj_prompt",
    )(*args)


def _sink_softmax_pv(q, kg, vg, valid, sink):
    s = lax.dot_general(q, kg, (((1,), (1,)), ((), ())), preferred_element_type=F32)
    s = jnp.where(valid, s, NEG)
    m = jnp.maximum(jnp.max(s, axis=1, keepdims=True), sink)
    p = jnp.exp(s - m)
    denom = jnp.sum(p, axis=1, keepdims=True) + jnp.exp(sink - m)
    return jnp.dot(p.astype(BF16), vg, preferred_element_type=F32) / denom


def _attn_prompt_kernel(sink_ref, q_ref, kc_ref, kp_ref, vc_ref, vp_ref, o_ref):
    j = pl.program_id(1)
    q = q_ref[...]
    kb = jnp.concatenate([kp_ref[...], kc_ref[...]], axis=0).astype(BF16)
    vb = jnp.concatenate([vp_ref[...], vc_ref[...]], axis=0).astype(BF16)
    qi = lax.broadcasted_iota(jnp.int32, (ATTN_BLOCK, 2 * ATTN_BLOCK), 0)
    kj = lax.broadcasted_iota(jnp.int32, (ATTN_BLOCK, 2 * ATTN_BLOCK), 1)
    valid = (kj >= qi) & (kj <= qi + WINDOW) & ((kj >= ATTN_BLOCK) | (j > 0))
    outs = []
    for h in range(N_HEADS):
        g = h // GROUP
        kg = kb[:, g * HEAD_DIM:(g + 1) * HEAD_DIM]
        vg = vb[:, g * HEAD_DIM:(g + 1) * HEAD_DIM]
        outs.append(_sink_softmax_pv(q[:, h * HEAD_DIM:(h + 1) * HEAD_DIM], kg, vg, valid, sink_ref[h]))
    o_ref[...] = jnp.concatenate(outs, axis=1).astype(BF16)


def _attn_prompt(q, k, v, sinks, batch, seq):
    nb = seq // ATTN_BLOCK
    cur = lambda w: pl.BlockSpec((ATTN_BLOCK, w), lambda b, j: (b * nb + j, 0))
    prev = lambda w: pl.BlockSpec((ATTN_BLOCK, w), lambda b, j: (b * nb + jnp.maximum(j - 1, 0), 0))
    return pl.pallas_call(
        _attn_prompt_kernel,
        grid=(batch, nb),
        in_specs=[pl.BlockSpec(memory_space=pltpu.SMEM), cur(Q_W), cur(KV_W), prev(KV_W), cur(KV_W), prev(KV_W)],
        out_specs=cur(Q_W),
        out_shape=jax.ShapeDtypeStruct((batch * seq, Q_W), BF16),
        compiler_params=_params(2),
        name="attn_prompt",
    )(sinks, q, k, k, v, v)


def _attn_sample_kernel(bb, n_new, sink_ref, q_ref, kn_ref, vn_ref, kc_ref, vc_ref, o_ref):
    rows = GROUP * n_new
    s_len = 2 * WINDOW
    ri = lax.broadcasted_iota(jnp.int32, (rows, s_len), 0)
    kj = lax.broadcasted_iota(jnp.int32, (rows, s_len), 1)
    t = ri % n_new
    valid = ((kj < WINDOW) & (kj >= t)) | ((kj >= WINDOW) & (kj - WINDOW <= t))
    pad = jnp.zeros((WINDOW - SUBLANES, KV_W), F32)
    for b in range(bb):
        kb = jnp.concatenate([kc_ref[b], kn_ref[b], pad], axis=0).astype(BF16)
        vb = jnp.concatenate([vc_ref[b], vn_ref[b], pad], axis=0).astype(BF16)
        for g in range(N_KV_HEADS):
            kg = kb[:, g * HEAD_DIM:(g + 1) * HEAD_DIM]
            vg = vb[:, g * HEAD_DIM:(g + 1) * HEAD_DIM]
            o_ref[b, g] = _sink_softmax_pv(q_ref[b, g], kg, vg, valid, sink_ref[:, g:g + 1]).astype(BF16)


def _attn_sample(q4, k_new, v_new, cache_k, cache_v, sink_mat, n_new):
    nbatch = q4.shape[0]
    bb = 8 if nbatch % 8 == 0 else nbatch
    rows = GROUP * n_new
    b3 = lambda r, w: pl.BlockSpec((bb, r, w), lambda i: (i, 0, 0))
    qspec = pl.BlockSpec((bb, N_KV_HEADS, rows, HEAD_DIM), lambda i: (i, 0, 0, 0))
    return pl.pallas_call(
        functools.partial(_attn_sample_kernel, bb, n_new),
        grid=(nbatch // bb,),
        in_specs=[_const_spec(sink_mat.shape), qspec, b3(SUBLANES, KV_W), b3(SUBLANES, KV_W),
                  b3(WINDOW, KV_W), b3(WINDOW, KV_W)],
        out_specs=qspec,
        out_shape=jax.ShapeDtypeStruct(q4.shape, BF16),
        compiler_params=_params(1),
        name="attn_sample",
    )(sink_mat, q4, k_new, v_new, cache_k, cache_v)


def _topk_rows(s, k, payload=None):
    r = s.shape[0]
    rows = lax.broadcasted_iota(jnp.int32, s.shape, 0)
    vals, sel = [], []
    for _ in range(k):
        m = jnp.max(s, axis=0, keepdims=True)
        ix = jnp.min(jnp.where(s == m, rows, r), axis=0, keepdims=True)
        hit = rows == ix
        vals.append(m)
        sel.append(ix if payload is None else jnp.max(jnp.where(hit, payload, -1), axis=0, keepdims=True))
        s = jnp.where(hit, -jnp.inf, s)
    return jnp.concatenate(vals, axis=0), jnp.concatenate(sel, axis=0)


def _peer_route(q0, q1, k0, k1):
    nt = (((1,), (1,)), ((), ()))
    s0 = lax.dot_general(k0, q0, nt, preferred_element_type=F32)
    s1 = lax.dot_general(k1, q1, nt, preferred_element_type=F32)
    sv0, si0 = _topk_rows(s0, PEER_TOPK)
    sv1, si1 = _topk_rows(s1, PEER_TOPK)
    tb = s0.shape[1]
    row8 = lax.broadcasted_iota(jnp.int32, (SUBLANES, tb), 0)
    n_grp = -(-len(PEER_PAIRS) // SUBLANES)
    cvs, ces = [], []
    for gi in range(n_grp):
        cv = jnp.full((SUBLANES, tb), -jnp.inf, F32)
        ce = jnp.zeros((SUBLANES, tb), jnp.int32)
        for r in range(SUBLANES):
            c = gi * SUBLANES + r
            if c < len(PEER_PAIRS):
                i, j = PEER_PAIRS[c]
                cv = jnp.where(row8 == r, sv0[i:i + 1, :] + sv1[j:j + 1, :], cv)
                ce = jnp.where(row8 == r, si0[i:i + 1, :] * N_KEYS + si1[j:j + 1, :], ce)
        cvs.append(cv)
        ces.append(ce)
    cs, e = _topk_rows(jnp.concatenate(cvs, axis=0), PEER_TOPK, payload=jnp.concatenate(ces, axis=0))
    p = jnp.exp(cs - cs[0:1, :])
    return e, p / jnp.sum(p, axis=0, keepdims=True)


def _post_kernel(tb, x_ref, at_ref, bc_ref, ga_ref, gc_ref, gt1_ref, sh2_ref, sc2_ref, n2_ref,
                 wa_ref, wc_ref, wo_ref, wpq_ref, sk_ref, x1_ref, h2_ref, e_ref, g_ref, qp_ref):
    merged = (ga_ref[...] * jnp.dot(at_ref[...], wa_ref[...], preferred_element_type=F32)
              + gc_ref[...] * jnp.dot(bc_ref[...], wc_ref[...], preferred_element_type=F32))
    x1 = x_ref[...] + gt1_ref[...] * jnp.dot(merged.astype(BF16), wo_ref[...], preferred_element_type=F32)
    x1_ref[...] = x1
    ms = jnp.mean(x1 * x1, axis=-1, keepdims=True)
    h2 = (x1 * lax.rsqrt(ms + EPS) * n2_ref[...]) * (1.0 + sc2_ref[...]) + sh2_ref[...]
    h2_ref[...] = h2
    qp = jnp.dot(h2.astype(BF16), wpq_ref[...], preferred_element_type=F32).astype(BF16)
    for hp in range(2 * PEER_HEADS):
        qp_ref[hp] = qp[:, hp * PEER_HALF:(hp + 1) * PEER_HALF]

    def head(h, carry):
        e, g = _peer_route(qp_ref[2 * h], qp_ref[2 * h + 1], sk_ref[2 * h], sk_ref[2 * h + 1])
        e_ref[h] = e
        g_ref[h] = g
        return carry

    lax.fori_loop(0, PEER_HEADS, head, 0)


def _post(sample, x, attn, bc, ga, gc, gt1, sh2, sc2, norm2, wa, wc, wo, wpq, sk, *, tb, steps_per_batch):
    t = x.shape[0]
    row = lambda w: pl.BlockSpec((tb, w), lambda i: (i, 0))
    if sample:
        mod_spec = pl.BlockSpec((tb, D_MODEL), lambda i: (0, 0))
    else:
        mod_spec = pl.BlockSpec((None, 1, D_MODEL), lambda i: (i // steps_per_batch, 0, 0))
    sel_spec = pl.BlockSpec((PEER_HEADS, PEER_TOPK, tb), lambda i: (0, 0, i))
    return pl.pallas_call(
        functools.partial(_post_kernel, tb),
        grid=(t // tb,),
        in_specs=[row(D_MODEL), row(Q_W), row(D_CONV), row(D_MODEL), row(D_MODEL), mod_spec, mod_spec, mod_spec,
                  _const_spec((1, D_MODEL)), _const_spec(wa.shape), _const_spec(wc.shape), _const_spec(wo.shape),
                  _const_spec(wpq.shape), _const_spec(sk.shape)],
        out_specs=[row(D_MODEL), row(D_MODEL), sel_spec, sel_spec],
        out_shape=[jax.ShapeDtypeStruct((t, D_MODEL), F32), jax.ShapeDtypeStruct((t, D_MODEL), F32),
                   jax.ShapeDtypeStruct((PEER_HEADS, PEER_TOPK, t), jnp.int32),
                   jax.ShapeDtypeStruct((PEER_HEADS, PEER_TOPK, t), F32)],
        scratch_shapes=[pltpu.VMEM((2 * PEER_HEADS, tb, PEER_HALF), BF16)],
        compiler_params=_params(1),
        name="post_sample" if sample else "post_prompt",
    )(x, attn, bc, ga, gc, gt1, sh2, sc2, norm2, wa, wc, wo, wpq, sk)


def _sublane_sums(p):
    sub = lax.broadcasted_iota(jnp.int32, (SUBLANES, LANES), 0)

    def merge(a, b, sh):
        lo = (sub & sh) == 0
        fa = a + pltpu.roll(a, SUBLANES - sh, axis=0)
        fb = b + pltpu.roll(b, sh, axis=0)
        return jnp.where(lo, fa, fb)

    n = [merge(p[i], p[i + 4], 4) for i in range(4)]
    q = [merge(n[0], n[2], 2), merge(n[1], n[3], 2)]
    return merge(q[0], q[1], 1)


def _u_half(word):
    return lax.bitcast_convert_type(word & jnp.int32(-65536), F32)


def _v_half(word):
    return lax.bitcast_convert_type(word << 16, F32)


def _peer_kernel(n, idx_cur, idx_nxt, h_ref, g_ref, uv_hbm, o_ref, buf0, buf1, wbuf, sem):
    i = pl.program_id(0)
    phase_rows = PEER_TOK * PEER_SEL
    bufs = (buf0, buf1)

    def token_rows(s, t):
        return bufs[s].at[pl.ds(t * PEER_SEL, PEER_SEL)]

    def issue_token(idx_ref, off, s, t):
        for k in range(PEER_SEL):
            r = t * PEER_SEL + k
            pltpu.make_async_copy(uv_hbm.at[idx_ref[off + r]], bufs[s].at[r], sem.at[s, t]).start(
                priority=k % PEER_DMA_QUEUES)

    def wait_token(s, t):
        pltpu.make_async_copy(uv_hbm.at[pl.ds(0, PEER_SEL)], token_rows(s, t), sem.at[s, t]).wait()

    @pl.when(i == 0)
    def _():
        def body(r, c):
            pltpu.make_async_copy(uv_hbm.at[idx_cur[r]], buf0.at[r], sem.at[0, r // PEER_SEL]).start()
            return c
        lax.fori_loop(0, phase_rows, body, 0)

    eye = (lax.broadcasted_iota(jnp.int32, (PEER_SEL, PEER_SEL), 0)
           == lax.broadcasted_iota(jnp.int32, (PEER_SEL, PEER_SEL), 1)).astype(F32)
    gcol = lax.dot_general(eye, g_ref[...], (((1,), (1,)), ((), ())), precision=HIGHEST, preferred_element_type=F32)
    for s in range(2):
        for t in range(PEER_TOK):
            wait_token(s, t)
            if s == 0:
                issue_token(idx_cur, phase_rows, 1, t)
            else:
                issue_token(idx_nxt, 0, 0, t)
            tok = s * PEER_TOK + t
            h = h_ref[tok]
            base = t * PEER_SEL
            parts = []
            for k0 in range(0, PEER_SEL, SUBLANES):
                parts.append(_sublane_sums([_u_half(bufs[s][base + k0 + j]) * h for j in range(SUBLANES)]))
            a = jnp.sum(jnp.concatenate(parts, axis=0), axis=1, keepdims=True)
            w = gcol[:, tok:tok + 1] * jax.nn.gelu(a)
            wbuf[...] = jnp.broadcast_to(w, (PEER_SEL, LANES))
            accs = [jnp.zeros((SUBLANES, LANES), F32) for _ in range(4)]
            for k in range(PEER_SEL):
                accs[k % 4] = accs[k % 4] + wbuf[k:k + 1, :] * _v_half(bufs[s][base + k])
            o_ref[tok] = (accs[0] + accs[1]) + (accs[2] + accs[3])

    @pl.when(i == n - 1)
    def _():
        for t in range(PEER_TOK):
            wait_token(0, t)


def _peer(idx_flat, h3, g_tok, uv3):
    t = h3.shape[0]
    step_tok = 2 * PEER_TOK
    n = t // step_tok
    rows = step_tok * PEER_SEL
    tile = pl.BlockSpec((step_tok, SUBLANES, LANES), lambda i: (i, 0, 0))
    return pl.pallas_call(
        functools.partial(_peer_kernel, n),
        grid=(n,),
        in_specs=[pl.BlockSpec((rows,), lambda i: (i,), memory_space=pltpu.SMEM),
                  pl.BlockSpec((rows,), lambda i: (jnp.minimum(i + 1, n - 1),), memory_space=pltpu.SMEM),
                  tile,
                  pl.BlockSpec((step_tok, PEER_SEL), lambda i: (i, 0)),
                  pl.BlockSpec(memory_space=pl.ANY)],
        out_specs=tile,
        out_shape=jax.ShapeDtypeStruct((t, SUBLANES, LANES), F32),
        scratch_shapes=[pltpu.VMEM((PEER_TOK * PEER_SEL, SUBLANES, LANES), jnp.int32),
                        pltpu.VMEM((PEER_TOK * PEER_SEL, SUBLANES, LANES), jnp.int32),
                        pltpu.VMEM((PEER_SEL, LANES), F32),
                        pltpu.SemaphoreType.DMA((2, PEER_TOK))],
        compiler_params=pltpu.CompilerParams(dimension_semantics=("arbitrary",), vmem_limit_bytes=VMEM_LIMIT,
                                             disable_bounds_checks=True),
        name="peer_mix",
    )(idx_flat, idx_flat, h3, g_tok, uv3)


def _gelu_tanh_via_exp(a):
    z = 0.7978845608028654 * (a + 0.044715 * (a * a * a))
    t = 1.0 - 2.0 / (jnp.exp(2.0 * z) + 1.0)
    return 0.5 * a * (1.0 + t)


def _sc_peer(tab, idx, h, g):
    t_all = idx.shape[0]
    per_w = t_all // SC_WORKERS
    n_pass = PEER_SEL // SC_PASS
    n_chunk = D_MODEL // SC_LANES
    quarter = 16 * SC_LANES
    mesh = plsc.VectorSubcoreMesh(core_axis_name="c", subcore_axis_name="s")

    @functools.partial(
        pl.kernel, mesh=mesh,
        out_type=jax.ShapeDtypeStruct((t_all, D_MODEL), F32),
        scratch_types=[pltpu.VMEM((2, PEER_SEL), jnp.int32),
                       pltpu.VMEM((2, SC_PASS, SUBLANES, LANES), jnp.int32),
                       pltpu.VMEM((2, D_MODEL), F32),
                       pltpu.VMEM((2, PEER_SEL), F32),
                       pltpu.VMEM((SC_PASS, SC_LANES), F32),
                       pltpu.VMEM((PEER_SEL,), F32),
                       pltpu.VMEM((D_MODEL,), F32),
                       pltpu.SemaphoreType.DMA((2,)),
                       pltpu.SemaphoreType.DMA((2,))],
        compiler_params=pltpu.CompilerParams(needs_layout_passes=False),
    )
    def k(tab_hbm, idx_hbm, h_hbm, g_hbm, out_hbm, idx_v, rows_v, h_v, g_v, ap_v, w_v, o_v, tsem, rsem):
        wid = lax.axis_index("s") * SC_CORES + lax.axis_index("c")
        lane = lax.iota(jnp.int32, SC_LANES)
        zero = jnp.zeros((SC_LANES,), F32)
        tok0 = wid * per_w

        def token_inputs(tok, slot):
            return (pltpu.make_async_copy(idx_hbm.at[tok], idx_v.at[slot], tsem.at[slot]),
                    pltpu.make_async_copy(h_hbm.at[tok], h_v.at[slot], tsem.at[slot]),
                    pltpu.make_async_copy(g_hbm.at[tok], g_v.at[slot], tsem.at[slot]))

        def gather(slot, q, b):
            ids = idx_v.at[slot].at[pl.ds(q * SC_PASS, SC_PASS)]
            return pltpu.make_async_copy(tab_hbm.at[ids], rows_v.at[b], rsem.at[b])

        def row_chunk(b, kk, jc, j):
            col = jc * quarter + j * SC_LANES
            return rows_v[b, kk, col // LANES, pl.ds(col % LANES, SC_LANES)]

        for c in token_inputs(tok0, 0):
            c.start()
        for c in token_inputs(tok0, 0):
            c.wait()
        gather(0, 0, 0).start()

        def token(ti, carry):
            ts = lax.rem(ti, 2)
            tn = 1 - ts
            tok = tok0 + ti
            nxt = tok0 + jnp.minimum(ti + 1, per_w - 1)
            for c in token_inputs(nxt, tn):
                c.start()
            for j in range(n_chunk):
                o_v[pl.ds(j * SC_LANES, SC_LANES)] = zero
            for q in range(n_pass):
                b = q % 2
                gather(ts, q, b).wait()
                if q + 1 < n_pass:
                    gather(ts, q + 1, 1 - b).start()
                else:
                    for c in token_inputs(nxt, tn):
                        c.wait()
                    gather(tn, 0, 1 - b).start()
                for jc in range(D_MODEL // quarter):
                    hregs = [h_v[ts, pl.ds(jc * quarter + j * SC_LANES, SC_LANES)] for j in range(16)]

                    def dot_rows(kk, jc=jc, hregs=hregs, b=b):
                        part = [None] * 4
                        for j in range(16):
                            p = _u_half(row_chunk(b, kk, jc, j)) * hregs[j]
                            part[j % 4] = p if part[j % 4] is None else part[j % 4] + p
                        acc = (part[0] + part[1]) + (part[2] + part[3])
                        if jc == 0:
                            ap_v[kk, :] = acc
                        else:
                            plsc.addupdate(ap_v.at[kk], acc)

                    plsc.parallel_loop(0, SC_PASS)(dot_rows)
                for m in range(SC_PASS // SC_LANES):
                    rowi = lane + m * SC_LANES
                    a = zero
                    for col in range(SC_LANES):
                        a = a + plsc.load_gather(ap_v, [rowi, jnp.full((SC_LANES,), col, jnp.int32)])
                    off = q * SC_PASS + m * SC_LANES
                    w_v[pl.ds(off, SC_LANES)] = g_v[ts, pl.ds(off, SC_LANES)] * _gelu_tanh_via_exp(a)
                for jc in range(D_MODEL // quarter):
                    def mix_rows(kk, accs, q=q, jc=jc, b=b):
                        ws = plsc.load_gather(w_v, [jnp.full((SC_LANES,), q * SC_PASS, jnp.int32) + kk])
                        return tuple(accs[j] + ws * _v_half(row_chunk(b, kk, jc, j)) for j in range(16))

                    accs = plsc.parallel_loop(0, SC_PASS, carry=tuple(zero for _ in range(16)))(mix_rows)
                    for j in range(16):
                        plsc.addupdate(o_v.at[pl.ds(jc * quarter + j * SC_LANES, SC_LANES)], accs[j])
            pltpu.sync_copy(o_v, out_hbm.at[tok])
            return carry

        lax.fori_loop(0, per_w, token, 0)
        gather(0, 0, 0).wait()

    return k(tab, idx, h, g)


def _final_kernel(x_ref, gt_ref, o_ref, y_ref):
    y_ref[...] = x_ref[...] + gt_ref[...] * o_ref[...]


def _final(sample, x1, gt2, out, *, tb, steps_per_batch):
    t = x1.shape[0]
    row = pl.BlockSpec((tb, D_MODEL), lambda i: (i, 0))
    if sample:
        mod_spec = pl.BlockSpec((tb, D_MODEL), lambda i: (0, 0))
    else:
        mod_spec = pl.BlockSpec((None, 1, D_MODEL), lambda i: (i // steps_per_batch, 0, 0))
    return pl.pallas_call(
        _final_kernel,
        grid=(t // tb,),
        in_specs=[row, mod_spec, row],
        out_specs=row,
        out_shape=jax.ShapeDtypeStruct((t, D_MODEL), F32),
        compiler_params=_params(1),
        name="final_sample" if sample else "final_prompt",
    )(x1, gt2, out)


def _rope_tables(pos):
    half = HEAD_DIM // 2
    inv = jnp.power(jnp.float32(ROPE_THETA), -jnp.arange(half, dtype=F32) / half)
    ang = pos.astype(F32)[:, None] * inv[None, :]
    lane = np.arange(LANES)
    sign = jnp.asarray(np.where((lane % HEAD_DIM) < half, -1.0, 1.0).astype(np.float32))
    cos = jnp.cos(ang)[:, lane % half]
    sin = jnp.sin(ang)[:, lane % half] * sign[None, :]
    return cos, sin


def _dense(sample, x, mods, pos, lw, *, tb, steps_per_batch, batch, seq, win_k=None, win_v=None, conv_state=None):
    sh1, sc1, gt1, sh2, sc2, _ = mods
    cos, sin = _rope_tables(pos)
    if sample:
        cos, sin = cos[:, None, :], sin[:, None, :]
    q, k, v, bc, ga, gc, u_out = _in_proj(sample, x, sh1, sc1, lw["norm1"], lw["w_in"], lw["qn"], lw["kn"], cos, sin,
                                          lw["conv_w"], conv_state, tb=tb, steps_per_batch=steps_per_batch)
    if sample:
        n_new, nb = seq, batch
        q4 = q.reshape(n_new, nb, N_KV_HEADS, GROUP, HEAD_DIM).transpose(1, 2, 3, 0, 4).reshape(
            nb, N_KV_HEADS, GROUP * n_new, HEAD_DIM)
        to_b = lambda a: jnp.pad(a.reshape(n_new, nb, KV_W).transpose(1, 0, 2), ((0, 0), (0, SUBLANES - n_new), (0, 0)))
        sink_mat = jnp.repeat(lw["sinks"].reshape(N_KV_HEADS, GROUP), n_new, axis=1).T
        o4 = _attn_sample(q4, to_b(k), to_b(v), win_k.reshape(nb, WINDOW, KV_W), win_v.reshape(nb, WINDOW, KV_W),
                          sink_mat, n_new)
        attn = o4.reshape(nb, N_KV_HEADS, GROUP, n_new, HEAD_DIM).transpose(3, 0, 1, 2, 4).reshape(n_new * nb, Q_W)
    else:
        attn = _attn_prompt(q, k, v, lw["sinks"], batch, seq)
    pt = 1 if sample or steps_per_batch % POST_TILES else POST_TILES
    x1, h2, e, g = _post(sample, x, attn, bc, ga, gc, gt1, sh2, sc2, lw["norm2"], lw["wa"], lw["wc"], lw["wo"],
                         lw["wpq"], lw["sk"], tb=tb * pt, steps_per_batch=steps_per_batch // pt)
    t = x.shape[0]
    return x1, h2, e.reshape(PEER_SEL, t).T, g.reshape(PEER_SEL, t).T, k, v, u_out


def _tc_peer(lw, idx, h2, g_tok):
    t = idx.shape[0]
    return _peer(idx.reshape(t * PEER_SEL), h2.reshape(t, SUBLANES, LANES), g_tok,
                 lw["uv"]).reshape(t, D_MODEL)


def _prompt_peer(lw, parts):
    unit = 2 * SC_WORKERS
    total = sum(p[0].shape[0] for p in parts)
    left = int(total * SC_TOKEN_SHARE) // unit * unit
    n_sc = [0] * len(parts)
    for i in reversed(range(len(parts))):
        n_sc[i] = min(left, parts[i][0].shape[0])
        left -= n_sc[i]
    sc_out = [_sc_peer(lw["uv"], idx[:n], h2[:n], g_tok[:n]) if n else None
              for (idx, h2, g_tok), n in zip(parts, n_sc)]
    outs = []
    for (idx, h2, g_tok), n, so in zip(parts, n_sc, sc_out):
        if so is not None:
            outs.append(so)
        if n < idx.shape[0]:
            outs.append(_tc_peer(lw, idx[n:], h2[n:], g_tok[n:]))
    return jnp.concatenate(outs, axis=0)


def _pack_uv(u, v):
    e = u.shape[0]
    per_w = e // SC_WORKERS
    rows = 2 * SUBLANES
    mesh = plsc.VectorSubcoreMesh(core_axis_name="c", subcore_axis_name="s")

    @functools.partial(
        pl.kernel, mesh=mesh,
        out_type=jax.ShapeDtypeStruct((e, SUBLANES, LANES), jnp.int32),
        scratch_types=[pltpu.VMEM((rows, D_MODEL), F32), pltpu.VMEM((rows, D_MODEL), F32),
                       pltpu.VMEM((rows, SUBLANES, LANES), jnp.int32)],
        compiler_params=pltpu.CompilerParams(needs_layout_passes=False),
    )
    def k(u_hbm, v_hbm, o_hbm, ub, vb, ob):
        wid = lax.axis_index("s") * SC_CORES + lax.axis_index("c")

        def chunk(c, carry):
            r0 = pl.multiple_of(wid * per_w + c * rows, SUBLANES)
            pltpu.sync_copy(u_hbm.at[pl.ds(r0, rows)], ub)
            pltpu.sync_copy(v_hbm.at[pl.ds(r0, rows)], vb)

            def pack_row(r):
                for j in range(D_MODEL // SC_LANES):
                    col = j * SC_LANES
                    word = plsc.pack(vb[r, pl.ds(col, SC_LANES)], ub[r, pl.ds(col, SC_LANES)],
                                     format=plsc.PackFormat.INTERLEAVED)
                    ob[r, col // LANES, pl.ds(col % LANES, SC_LANES)] = plsc.bitcast(word, jnp.int32)

            plsc.parallel_loop(0, rows)(pack_row)
            pltpu.sync_copy(ob, o_hbm.at[pl.ds(r0, rows)])
            return carry

        lax.fori_loop(0, per_w // rows, chunk, 0)

    return k(u, v)


def kernel(x_prompt, x_sample, c_prompt, c_sample, cache_win_k, cache_win_v, state_conv, w_ada, b_ada, norm1, w_in, q_norm, k_norm, attn_sinks, conv_w, w_attn_o, w_conv_o, w_out, norm2, w_peer_q, peer_sub_keys, peer_u, peer_v):
    depth = w_ada.shape[0]
    bp, sp, _ = x_prompt.shape
    bs, ss, _ = x_sample.shape
    assert sp % ATTN_BLOCK == 0 and ss <= SUBLANES and (bs * ss) % (2 * PEER_TOK) == 0
    tb_p = 256 if sp % 256 == 0 else ATTN_BLOCK
    pos_p = jnp.arange(sp)
    pos_s = PAST_LEN + jnp.arange(ss)
    yp = x_prompt.reshape(bp * sp, D_MODEL)
    ys = x_sample.transpose(1, 0, 2).reshape(ss * bs, D_MODEL)
    kp_l, vp_l, cp_l, ks_l, vs_l, cs_l = [], [], [], [], [], []
    for l in range(depth):
        lw = {
            "norm1": norm1[l].reshape(1, D_MODEL), "norm2": norm2[l].reshape(1, D_MODEL),
            "w_in": w_in[l].astype(BF16),
            "qn": jnp.tile(q_norm[l], N_HEADS).reshape(1, Q_W), "kn": jnp.tile(k_norm[l], N_KV_HEADS).reshape(1, KV_W),
            "sinks": attn_sinks[l], "conv_w": conv_w[l],
            "wa": w_attn_o[l].astype(BF16), "wc": w_conv_o[l].astype(BF16), "wo": w_out[l].astype(BF16),
            "wpq": w_peer_q[l].astype(BF16),
            "sk": peer_sub_keys[l].reshape(2 * PEER_HEADS, N_KEYS, PEER_HALF).astype(BF16),
        }
        c_all = jnp.concatenate([c_prompt, c_sample], axis=0)
        pad = (-c_all.shape[0]) % SUBLANES
        mod = _mod(jnp.pad(c_all, ((0, pad), (0, 0))), w_ada[l], b_ada[l])
        mods_p = tuple(m.reshape(bp, 1, D_MODEL) for m in jnp.split(mod[:bp], 6, axis=-1))
        mods_s = tuple(jnp.split(mod[bp:bp + bs], 6, axis=-1))
        x1, h2, idx, g_tok, ksn, vsn, us = _dense(True, ys, mods_s, pos_s, lw, tb=bs, steps_per_batch=1, batch=bs,
                                                  seq=ss, win_k=cache_win_k[l], win_v=cache_win_v[l],
                                                  conv_state=state_conv[l].transpose(1, 0, 2))
        seqs = [_dense(False, yp[b * sp:(b + 1) * sp], tuple(m[b:b + 1] for m in mods_p), pos_p, lw, tb=tb_p,
                       steps_per_batch=sp // tb_p, batch=1, seq=sp) for b in range(bp)]
        lw["uv"] = _pack_uv(peer_u[l], peer_v[l])
        ys = _final(True, x1, mods_s[5], _tc_peer(lw, idx, h2, g_tok), tb=bs, steps_per_batch=1)
        out = _prompt_peer(lw, [(s[2], s[1], s[3]) for s in seqs])
        yp = _final(False, jnp.concatenate([s[0] for s in seqs], axis=0), mods_p[5], out, tb=tb_p,
                    steps_per_batch=sp // tb_p)
        kp_l.append(jnp.stack([s[4][-WINDOW:].reshape(WINDOW, N_KV_HEADS, HEAD_DIM) for s in seqs]))
        vp_l.append(jnp.stack([s[5][-WINDOW:].reshape(WINDOW, N_KV_HEADS, HEAD_DIM) for s in seqs]))
        cp_l.append(jnp.stack([s[6][-(CONV_WIDTH - 1):] for s in seqs]))
        k_new = ksn.reshape(ss, bs, N_KV_HEADS, HEAD_DIM).transpose(1, 0, 2, 3)
        v_new = vsn.reshape(ss, bs, N_KV_HEADS, HEAD_DIM).transpose(1, 0, 2, 3)
        ks_l.append(jnp.concatenate([cache_win_k[l], k_new], axis=1)[:, -WINDOW:])
        vs_l.append(jnp.concatenate([cache_win_v[l], v_new], axis=1)[:, -WINDOW:])
        ext = jnp.concatenate([state_conv[l], us.reshape(ss, bs, D_CONV).transpose(1, 0, 2)], axis=1)
        cs_l.append(ext[:, -(CONV_WIDTH - 1):])
    return (yp.reshape(bp, sp, D_MODEL), ys.reshape(ss, bs, D_MODEL).transpose(1, 0, 2),
            jnp.stack(kp_l), jnp.stack(vp_l), jnp.stack(cp_l), jnp.stack(ks_l), jnp.stack(vs_l), jnp.stack(cs_l))
```
